```python
import math
import jax, jax.numpy as jnp
from jax import lax
import numpy as np

D_MODEL = 4096
BATCH = 2
SEQ = 4096
DEPTH = 1
DEC_BATCH = 32
DEC_SEQ = 8
PAST_LEN = 8192
PAGE_SIZE = 128

ATTN_WIDTH = D_MODEL // 2
HALF_DIM = 64
QK_DIM = 2 * HALF_DIM
V_DIM = 2 * HALF_DIM
N_HEADS = ATTN_WIDTH // V_DIM
POOL_WIDTH = D_MODEL - ATTN_WIDTH
POOL_WINDOWS = (2, 4, 8, 16)
N_POOL_GROUPS = len(POOL_WINDOWS)
POOL_GROUP = POOL_WIDTH // N_POOL_GROUPS
POOL_BUF = max(POOL_WINDOWS) - 1
IN_WIDTH = 3 * ATTN_WIDTH + POOL_WIDTH
D_FF = ((8 * D_MODEL // 3 + 255) // 256) * 256
ROPE_THETA = 10000.0
Q_BLOCK = 128
N_ADA = 9
EPS = 1e-6
NEG = -1e30

kernel_name = 'hybrid_diffattn_pool_macaron_step'


def _rms(x, g):
    xf = x.astype(jnp.float32)
    y = xf * lax.rsqrt(jnp.mean(xf * xf, axis=-1, keepdims=True) + EPS)
    return (y * g.astype(jnp.float32)).astype(x.dtype)


def _rope(x, pos):
    inv = 1.0 / (ROPE_THETA ** (jnp.arange(0, HALF_DIM, 2, dtype=jnp.float32) / HALF_DIM))
    ang = pos.astype(jnp.float32)[:, None] * inv[None, :]
    cos = jnp.cos(ang)[:, None, None, :]
    sin = jnp.sin(ang)[:, None, None, :]
    xf = x.astype(jnp.float32)
    x1, x2 = xf[..., : HALF_DIM // 2], xf[..., HALF_DIM // 2:]
    return jnp.concatenate([x1 * cos - x2 * sin, x2 * cos + x1 * sin], axis=-1).astype(x.dtype)


def _swiglu(h, w1, w3, w2):
    return (jax.nn.silu(h @ w1) * (h @ w3)) @ w2


def _prompt_diff_attn(q, k, v, lam):
    B, S = q.shape[0], q.shape[1]
    n_blocks = S // Q_BLOCK
    key_pos = jnp.arange(S)
    scale = HALF_DIM ** -0.5
    vf = v.astype(jnp.float32)

    def block(i):
        qb = lax.dynamic_slice_in_dim(q, i * Q_BLOCK, Q_BLOCK, axis=1)
        s = jnp.einsum('bqhcd,bkhcd->bhcqk', qb, k, preferred_element_type=jnp.float32) * scale
        q_pos = i * Q_BLOCK + jnp.arange(Q_BLOCK)
        s = jnp.where(key_pos[None, :] <= q_pos[:, None], s, NEG)
        p = jax.nn.softmax(s, axis=-1)
        a = p[:, :, 0] - lam * p[:, :, 1]
        return jnp.einsum('bhqk,bkhd->bqhd', a, vf)

    out = lax.map(block, jnp.arange(n_blocks))
    return jnp.moveaxis(out, 0, 1).reshape(B, S, N_HEADS, V_DIM)


def _sample_diff_attn(q, k_new, v_new, cache_k, cache_v, page_table, lam):
    DB, T = q.shape[0], q.shape[1]
    qf = q.astype(jnp.float32) * (HALF_DIM ** -0.5)

    def update(carry, k, v, mask):
        m, l, acc = carry
        s = jnp.einsum('bqhcd,bkhcd->bhcqk', qf, k.astype(jnp.float32))
        if mask is not None:
            s = jnp.where(mask, s, NEG)
        m_new = jnp.maximum(m, s.max(axis=-1))
        corr = jnp.exp(m - m_new)
        p = jnp.exp(s - m_new[..., None])
        if mask is not None:
            p = jnp.where(mask, p, 0.0)
        l = l * corr + p.sum(axis=-1)
        acc = acc * corr[..., None] + jnp.einsum('bhcqk,bkhd->bhcqd', p, v.astype(jnp.float32))
        return (m_new, l, acc)

    def page_step(carry, phys):
        k = cache_k[phys].reshape(DB, PAGE_SIZE, N_HEADS, 2, HALF_DIM)
        v = cache_v[phys]
        return update(carry, k, v, None), None

    init = (jnp.full((DB, N_HEADS, 2, T), NEG, jnp.float32),
            jnp.zeros((DB, N_HEADS, 2, T), jnp.float32),
            jnp.zeros((DB, N_HEADS, 2, T, V_DIM), jnp.float32))
    carry, _ = lax.scan(page_step, init, page_table.T)
    tpos = jnp.arange(T)
    causal = tpos[None, :] <= tpos[:, None]
    _, l, acc = update(carry, k_new, v_new, causal)
    o = acc / l[..., None]
    a = o[:, :, 0] - lam * o[:, :, 1]
    return jnp.transpose(a, (0, 2, 1, 3))


def _multiscale_pool(u, buf, pos0):
    B, T, C = u.shape
    ext = jnp.concatenate([buf, u], axis=1).astype(jnp.float32)
    cs = jnp.concatenate([jnp.zeros((B, 1, C), jnp.float32), jnp.cumsum(ext, axis=1)], axis=1)
    pos = pos0 + jnp.arange(T)
    outs = []
    for g, w in enumerate(POOL_WINDOWS):
        sl = slice(g * POOL_GROUP, (g + 1) * POOL_GROUP)
        end = cs[:, POOL_BUF + 1: POOL_BUF + 1 + T, sl]
        start = cs[:, POOL_BUF + 1 - w: POOL_BUF + 1 - w + T, sl]
        cnt = jnp.minimum(pos + 1, w).astype(jnp.float32)[None, :, None]
        outs.append((end - start) / cnt - ext[:, POOL_BUF:, sl])
    new_buf = jnp.concatenate([buf, u], axis=1)[:, -POOL_BUF:]
    return jnp.stack(outs, axis=2), new_buf


def _mixer(h, pos0, attend, pool_buf, p, lam_init):
    B, T, _ = h.shape
    proj = h @ p['w_in']
    q = proj[..., :ATTN_WIDTH].reshape(B, T, N_HEADS, 2, HALF_DIM)
    k = proj[..., ATTN_WIDTH: 2 * ATTN_WIDTH].reshape(B, T, N_HEADS, 2, HALF_DIM)
    v = proj[..., 2 * ATTN_WIDTH: 3 * ATTN_WIDTH].reshape(B, T, N_HEADS, V_DIM)
    u = proj[..., 3 * ATTN_WIDTH:]
    pos = pos0 + jnp.arange(T)
    q = _rope(q, pos)
    k = _rope(k, pos)
    lam = (jnp.exp(jnp.sum(p['lambda_q1'].astype(jnp.float32) * p['lambda_k1'].astype(jnp.float32)))
           - jnp.exp(jnp.sum(p['lambda_q2'].astype(jnp.float32) * p['lambda_k2'].astype(jnp.float32)))
           + lam_init)
    attn = attend(q, k, v, lam)
    attn = (_rms(attn, p['subln_g']) * (1.0 - lam_init)).reshape(B, T, ATTN_WIDTH).astype(h.dtype)
    pooled, new_buf = _multiscale_pool(u, pool_buf, pos0)
    pooled = jnp.einsum('btgc,gcd->btgd', pooled, p['w_pool_mix'].astype(jnp.float32))
    pooled = (pooled.reshape(B, T, POOL_WIDTH) * p['pool_scale'].astype(jnp.float32)).astype(h.dtype)
    a_up = attn @ p['w_up_attn']
    p_up = pooled @ p['w_up_pool']
    gates = jax.nn.sigmoid((h @ p['w_merge_gate']).astype(jnp.float32)).reshape(B, T, 2, D_MODEL)
    merged = (gates[:, :, 0] * a_up.astype(jnp.float32) + gates[:, :, 1] * p_up.astype(jnp.float32)).astype(h.dtype)
    out = merged @ p['w_out']
    k_rows = k.reshape(B, T, N_HEADS, QK_DIM)
    return out, k_rows, v, new_buf


def _layer(x, c, pos0, attend, pool_buf, p, lam_init):
    B = x.shape[0]
    ada = (jax.nn.silu(c) @ p['w_ada'] + p['b_ada']).reshape(B, 1, N_ADA, D_MODEL)

    def mod(h, j):
        return h * (1.0 + ada[:, :, 3 * j + 1]) + ada[:, :, 3 * j]

    h = mod(_rms(x, p['g_pre_ffn1']), 0)
    f = _swiglu(h, p['w1_ffn1'], p['w3_ffn1'], p['w2_ffn1'])
    x = x + 0.5 * ada[:, :, 2] * _rms(f, p['g_post_ffn1'])

    h = mod(_rms(x, p['g_pre_mix']), 1)
    m, k_rows, v_rows, new_buf = _mixer(h, pos0, attend, pool_buf, p, lam_init)
    x = x + ada[:, :, 5] * _rms(m, p['g_post_mix'])

    h = mod(_rms(x, p['g_pre_ffn2']), 2)
    f = _swiglu(h, p['w1_ffn2'], p['w3_ffn2'], p['w2_ffn2'])
    x = x + 0.5 * ada[:, :, 8] * _rms(f, p['g_post_ffn2'])
    return x, k_rows, v_rows, new_buf


def setup_inputs(seed: int = 0) -> dict:
    key = jax.random.key(seed)
    ks = jax.random.split(key, 40)
    f32 = jnp.float32
    n_pages = PAST_LEN // PAGE_SIZE
    n_phys = (5 * DEC_BATCH * n_pages) // 4

    def nrm(k, shape, s):
        return jax.random.normal(k, shape, f32) * s

    def gain(k, n):
        return 1.0 + 0.05 * jax.random.normal(k, (n,), f32)

    page_table = jax.random.permutation(ks[5], n_phys)[: DEC_BATCH * n_pages].reshape(DEC_BATCH, n_pages).astype(jnp.int32)
    return {
        'x_prompt': nrm(ks[0], (BATCH, SEQ, D_MODEL), 1.0),
        'x_sample': nrm(ks[1], (DEC_BATCH, DEC_SEQ, D_MODEL), 1.0),
        'cache_k': nrm(ks[2], (n_phys, PAGE_SIZE, N_HEADS, QK_DIM), 1.0),
        'cache_v': nrm(ks[3], (n_phys, PAGE_SIZE, N_HEADS, V_DIM), 1.0),
        'state_pool': nrm(ks[4], (DEC_BATCH, POOL_BUF, POOL_WIDTH), 1.0),
        'page_table': page_table,
        'c_prompt': nrm(ks[6], (BATCH, D_MODEL), 1.0),
        'c_sample': nrm(ks[7], (DEC_BATCH, D_MODEL), 1.0),
        'w_ada': nrm(ks[8], (D_MODEL, N_ADA * D_MODEL), 0.5 * D_MODEL ** -0.5),
        'b_ada': nrm(ks[9], (N_ADA * D_MODEL,), 0.02),
        'g_pre_ffn1': gain(ks[10], D_MODEL),
        'w1_ffn1': nrm(ks[11], (D_MODEL, D_FF), D_MODEL ** -0.5),
        'w3_ffn1': nrm(ks[12], (D_MODEL, D_FF), D_MODEL ** -0.5),
        'w2_ffn1': nrm(ks[13], (D_FF, D_MODEL), D_FF ** -0.5),
        'g_post_ffn1': gain(ks[14], D_MODEL),
        'g_pre_mix': gain(ks[15], D_MODEL),
        'w_in': nrm(ks[16], (D_MODEL, IN_WIDTH), D_MODEL ** -0.5),
        'lambda_q1': nrm(ks[17], (HALF_DIM,), 0.1),
        'lambda_k1': nrm(ks[18], (HALF_DIM,), 0.1),
        'lambda_q2': nrm(ks[19], (HALF_DIM,), 0.1),
        'lambda_k2': nrm(ks[20], (HALF_DIM,), 0.1),
        'subln_g': gain(ks[21], V_DIM),
        'w_pool_mix': nrm(ks[22], (N_POOL_GROUPS, POOL_GROUP, POOL_GROUP), POOL_GROUP ** -0.5),
        'pool_scale': gain(ks[23], POOL_WIDTH),
        'w_up_attn': nrm(ks[24], (ATTN_WIDTH, D_MODEL), ATTN_WIDTH ** -0.5),
        'w_up_pool': nrm(ks[25], (POOL_WIDTH, D_MODEL), POOL_WIDTH ** -0.5),
        'w_merge_gate': nrm(ks[26], (D_MODEL, 2 * D_MODEL), D_MODEL ** -0.5),
        'w_out': nrm(ks[27], (D_MODEL, D_MODEL), D_MODEL ** -0.5),
        'g_post_mix': gain(ks[28], D_MODEL),
        'g_pre_ffn2': gain(ks[29], D_MODEL),
        'w1_ffn2': nrm(ks[30], (D_MODEL, D_FF), D_MODEL ** -0.5),
        'w3_ffn2': nrm(ks[31], (D_MODEL, D_FF), D_MODEL ** -0.5),
        'w2_ffn2': nrm(ks[32], (D_FF, D_MODEL), D_FF ** -0.5),
        'g_post_ffn2': gain(ks[33], D_MODEL),
    }


def reference(x_prompt, x_sample, cache_k, cache_v, state_pool, page_table, c_prompt, c_sample,
              w_ada, b_ada, g_pre_ffn1, w1_ffn1, w3_ffn1, w2_ffn1, g_post_ffn1,
              g_pre_mix, w_in, lambda_q1, lambda_k1, lambda_q2, lambda_k2, subln_g,
              w_pool_mix, pool_scale, w_up_attn, w_up_pool, w_merge_gate, w_out, g_post_mix,
              g_pre_ffn2, w1_ffn2, w3_ffn2, w2_ffn2, g_post_ffn2):
    p = dict(w_ada=w_ada, b_ada=b_ada, g_pre_ffn1=g_pre_ffn1, w1_ffn1=w1_ffn1, w3_ffn1=w3_ffn1,
             w2_ffn1=w2_ffn1, g_post_ffn1=g_post_ffn1, g_pre_mix=g_pre_mix, w_in=w_in,
             lambda_q1=lambda_q1, lambda_k1=lambda_k1, lambda_q2=lambda_q2, lambda_k2=lambda_k2,
             subln_g=subln_g, w_pool_mix=w_pool_mix, pool_scale=pool_scale, w_up_attn=w_up_attn,
             w_up_pool=w_up_pool, w_merge_gate=w_merge_gate, w_out=w_out, g_post_mix=g_post_mix,
             g_pre_ffn2=g_pre_ffn2, w1_ffn2=w1_ffn2, w3_ffn2=w3_ffn2, w2_ffn2=w2_ffn2,
             g_post_ffn2=g_post_ffn2)

    def prompt_attend(q, k, v, lam):
        return _prompt_diff_attn(q, k, v, lam)

    def sample_attend(q, k, v, lam):
        return _sample_diff_attn(q, k, v, cache_k, cache_v, page_table, lam)

    xp, xs = x_prompt, x_sample
    zero_buf = jnp.zeros((x_prompt.shape[0], POOL_BUF, POOL_WIDTH), x_prompt.dtype)
    for layer in range(DEPTH):
        lam_init = 0.8 - 0.6 * math.exp(-0.3 * layer)
        xp, k_prompt, v_prompt, pool_prompt = _layer(xp, c_prompt, 0, prompt_attend, zero_buf, p, lam_init)
        xs, k_sample, v_sample, pool_sample = _layer(xs, c_sample, PAST_LEN, sample_attend, state_pool, p, lam_init)
    return (xp, xs, k_prompt, v_prompt, pool_prompt, k_sample, v_sample, pool_sample)
```

```python
import functools
import math

import jax
import jax.numpy as jnp
from jax import lax
from jax.experimental import pallas as pl
from jax.experimental.pallas import tpu as pltpu

F32 = jnp.float32
BF16 = jnp.bfloat16

D_MODEL = 4096
BATCH = 2
SEQ = 4096
DEC_BATCH = 32
DEC_SEQ = 8
PAST_LEN = 8192
PAGE_SIZE = 128
N_PAGES = PAST_LEN // PAGE_SIZE
ATTN_WIDTH = D_MODEL // 2
HALF_DIM = 64
HEAD_DIM = 2 * HALF_DIM
N_HEADS = ATTN_WIDTH // HEAD_DIM
POOL_WIDTH = D_MODEL - ATTN_WIDTH
POOL_WINDOWS = (2, 4, 8, 16)
POOL_GROUP = POOL_WIDTH // len(POOL_WINDOWS)
POOL_BUF = max(POOL_WINDOWS) - 1
D_FF = ((8 * D_MODEL // 3 + 255) // 256) * 256
ROPE_THETA = 10000.0
N_ADA = 9
EPS = 1e-6
NEG = -1e30
LAM_INIT = 0.8 - 0.6 * math.exp(-0.3 * 0)

M_PROMPT = BATCH * SEQ
M_SAMPLE = DEC_BATCH * DEC_SEQ
M_ALL = M_PROMPT + M_SAMPLE

VMEM_LIMIT_BYTES = 58 * 1024 * 1024

ROW_TILE = 128
MM_ROWS = 768
MERGE_ROWS = 384
PAGES_PER_STEP = 4
HEADS_PER_GROUP = 8
Q_TILE = 512


def _params(sem):
    return pltpu.CompilerParams(dimension_semantics=sem, vmem_limit_bytes=VMEM_LIMIT_BYTES)


def _sigmoid(x):
    return 1.0 / (1.0 + jnp.exp(-x))


def _rms(x, g):
    return x * lax.rsqrt(jnp.mean(x * x, axis=-1, keepdims=True) + EPS) * g


def _ada_kernel(c_ref, w_ref, b_ref, o_ref):
    c = c_ref[...]
    a = (c * _sigmoid(c)).astype(BF16)
    o_ref[...] = jnp.dot(a, w_ref[...].astype(BF16), preferred_element_type=F32) + b_ref[...]


def _ada(c_pad, w_ada, b_ada):
    rows = c_pad.shape[0]
    n = w_ada.shape[1]
    tn = 1024
    return pl.pallas_call(
        _ada_kernel,
        grid=(n // tn,),
        in_specs=[pl.BlockSpec((rows, D_MODEL), lambda j: (0, 0)),
                  pl.BlockSpec((D_MODEL, tn), lambda j: (0, j)),
                  pl.BlockSpec((1, tn), lambda j: (0, j))],
        out_specs=pl.BlockSpec((rows, tn), lambda j: (0, j)),
        out_shape=jax.ShapeDtypeStruct((rows, n), F32),
        compiler_params=_params(("arbitrary",)),
    )(c_pad, w_ada, b_ada.reshape(1, n))


_PROMPT_TILES = M_PROMPT // ROW_TILE
_TILES_PER_BATCH = SEQ // ROW_TILE


def _mod_group(i):
    return jnp.where(i < _PROMPT_TILES, i // _TILES_PER_BATCH, i - _PROMPT_TILES + BATCH)


def _mod_spec(j):
    return pl.BlockSpec((1, 1, ROW_TILE, D_MODEL), lambda i: (j, _mod_group(i), 0, 0))


def _row_spec():
    return pl.BlockSpec((ROW_TILE, D_MODEL), lambda i: (i, 0))


def _vec_spec():
    return pl.BlockSpec((1, D_MODEL), lambda i: (0, 0))


def _norm_mod_kernel(x_ref, g_ref, scale_ref, shift_ref, h_ref):
    h = _rms(x_ref[...], g_ref[...])
    h_ref[...] = (h * (1.0 + scale_ref[0, 0]) + shift_ref[0, 0]).astype(h_ref.dtype)


def _norm_mod(x, g, mod, j):
    return pl.pallas_call(
        _norm_mod_kernel,
        grid=(M_ALL // ROW_TILE,),
        in_specs=[_row_spec(), _vec_spec(), _mod_spec(3 * j + 1), _mod_spec(3 * j)],
        out_specs=_row_spec(),
        out_shape=jax.ShapeDtypeStruct((M_ALL, D_MODEL), BF16),
        compiler_params=_params(("arbitrary",)),
    )(x, g.reshape(1, D_MODEL), mod, mod)


def _resid_norm_mod_kernel(x_ref, f_ref, gpost_ref, gate_ref, gpre_ref, scale_ref, shift_ref,
                           xo_ref, h_ref, *, coef):
    x = x_ref[...] + coef * gate_ref[0, 0] * _rms(f_ref[...], gpost_ref[...])
    xo_ref[...] = x
    h = _rms(x, gpre_ref[...])
    h_ref[...] = (h * (1.0 + scale_ref[0, 0]) + shift_ref[0, 0]).astype(h_ref.dtype)


def _resid_norm_mod(x, f, gpost, gpre, mod, j_prev, j_next, coef):
    return pl.pallas_call(
        functools.partial(_resid_norm_mod_kernel, coef=coef),
        grid=(M_ALL // ROW_TILE,),
        in_specs=[_row_spec(), _row_spec(), _vec_spec(), _mod_spec(3 * j_prev + 2), _vec_spec(),
                  _mod_spec(3 * j_next + 1), _mod_spec(3 * j_next)],
        out_specs=[_row_spec(), _row_spec()],
        out_shape=[jax.ShapeDtypeStruct((M_ALL, D_MODEL), F32),
                   jax.ShapeDtypeStruct((M_ALL, D_MODEL), BF16)],
        compiler_params=_params(("arbitrary",)),
    )(x, f, gpost.reshape(1, D_MODEL), mod, gpre.reshape(1, D_MODEL), mod, mod)


def _resid_kernel(x_ref, f_ref, gpost_ref, gate_ref, xo_ref, *, coef):
    xo_ref[...] = x_ref[...] + coef * gate_ref[0, 0] * _rms(f_ref[...], gpost_ref[...])


def _resid(x, f, gpost, mod, j_prev, coef):
    return pl.pallas_call(
        functools.partial(_resid_kernel, coef=coef),
        grid=(M_ALL // ROW_TILE,),
        in_specs=[_row_spec(), _row_spec(), _vec_spec(), _mod_spec(3 * j_prev + 2)],
        out_specs=_row_spec(),
        out_shape=jax.ShapeDtypeStruct((M_ALL, D_MODEL), F32),
        compiler_params=_params(("arbitrary",)),
    )(x, f, gpost.reshape(1, D_MODEL), mod)


def _swiglu_up_kernel(a_ref, w1_ref, w3_ref, o_ref, w1q, w3q):
    @pl.when(pl.program_id(1) == 0)
    def _():
        w1q[...] = w1_ref[...].astype(BF16)
        w3q[...] = w3_ref[...].astype(BF16)

    a = a_ref[...]
    u = jnp.dot(a, w1q[...], preferred_element_type=F32)
    v = jnp.dot(a, w3q[...], preferred_element_type=F32)
    o_ref[...] = (u * _sigmoid(u) * v).astype(o_ref.dtype)


def _swiglu_up(h, w1, w3):
    tn = 256
    tm = MM_ROWS
    return pl.pallas_call(
        _swiglu_up_kernel,
        grid=(D_FF // tn, M_ALL // tm),
        in_specs=[pl.BlockSpec((tm, D_MODEL), lambda j, i: (i, 0)),
                  pl.BlockSpec((D_MODEL, tn), lambda j, i: (0, j)),
                  pl.BlockSpec((D_MODEL, tn), lambda j, i: (0, j))],
        out_specs=pl.BlockSpec((tm, tn), lambda j, i: (i, j)),
        out_shape=jax.ShapeDtypeStruct((M_ALL, D_FF), BF16),
        scratch_shapes=[pltpu.VMEM((D_MODEL, tn), BF16), pltpu.VMEM((D_MODEL, tn), BF16)],
        compiler_params=_params(("arbitrary", "arbitrary")),
    )(h, w1, w3)


def _down_kernel(a_ref, w_ref, o_ref):
    part = jnp.dot(a_ref[...], w_ref[...].astype(BF16), preferred_element_type=F32)

    @pl.when(pl.program_id(1) == 0)
    def _():
        o_ref[...] = part

    @pl.when(pl.program_id(1) != 0)
    def _():
        o_ref[...] += part


def _down(g, w2):
    tk = 256
    tm = MM_ROWS
    return pl.pallas_call(
        _down_kernel,
        grid=(M_ALL // tm, D_FF // tk),
        in_specs=[pl.BlockSpec((tm, tk), lambda i, k: (i, k)),
                  pl.BlockSpec((tk, D_MODEL), lambda i, k: (k, 0))],
        out_specs=pl.BlockSpec((tm, D_MODEL), lambda i, k: (i, 0)),
        out_shape=jax.ShapeDtypeStruct((M_ALL, D_MODEL), F32),
        compiler_params=_params(("arbitrary", "arbitrary")),
    )(g, w2)


def _proj_kernel(a_ref, w_ref, *rest, rope, out_scale):
    if rope:
        cos_ref, sin_ref, o_ref, wq = rest
    else:
        o_ref, wq = rest

    @pl.when(pl.program_id(1) == 0)
    def _():
        wq[...] = w_ref[...].astype(BF16)

    y = jnp.dot(a_ref[...], wq[...], preferred_element_type=F32)
    if rope:
        tn = y.shape[1]
        reps = tn // HEAD_DIM
        cos = jnp.tile(cos_ref[...], (1, reps))
        sin = jnp.tile(sin_ref[...], (1, reps))
        lane = lax.broadcasted_iota(jnp.int32, y.shape, 1)
        first = (lane % HALF_DIM) < (HALF_DIM // 2)
        partner = jnp.where(first, pltpu.roll(y, tn - HALF_DIM // 2, 1), pltpu.roll(y, HALF_DIM // 2, 1))
        y = y * cos + partner * sin
    if out_scale != 1.0:
        y = y * out_scale
    o_ref[...] = y.astype(o_ref.dtype)


def _proj(a, w, col0, n, out_dtype, rope_tabs=None, out_scale=1.0):
    k = a.shape[1]
    tn = 512
    tm = MM_ROWS
    joff = col0 // tn
    in_specs = [pl.BlockSpec((tm, k), lambda j, i: (i, 0)),
                pl.BlockSpec((k, tn), lambda j, i: (0, j + joff))]
    args = [a, w]
    if rope_tabs is not None:
        in_specs += [pl.BlockSpec((tm, HEAD_DIM), lambda j, i: (i, 0))] * 2
        args += list(rope_tabs)
    return pl.pallas_call(
        functools.partial(_proj_kernel, rope=rope_tabs is not None, out_scale=out_scale),
        grid=(n // tn, M_ALL // tm),
        in_specs=in_specs,
        out_specs=pl.BlockSpec((tm, tn), lambda j, i: (i, j)),
        out_shape=jax.ShapeDtypeStruct((M_ALL, n), out_dtype),
        scratch_shapes=[pltpu.VMEM((k, tn), BF16)],
        compiler_params=_params(("arbitrary", "arbitrary")),
    )(*args)


def _merge_kernel(h_ref, at_ref, po_ref, wg0_ref, wg1_ref, wa_ref, wp_ref, o_ref,
                  wg0q, wg1q, waq, wpq):
    @pl.when(pl.program_id(1) == 0)
    def _():
        wg0q[...] = wg0_ref[...].astype(BF16)
        wg1q[...] = wg1_ref[...].astype(BF16)
        waq[...] = wa_ref[...].astype(BF16)
        wpq[...] = wp_ref[...].astype(BF16)

    h = h_ref[...]
    g0 = _sigmoid(jnp.dot(h, wg0q[...], preferred_element_type=F32))
    g1 = _sigmoid(jnp.dot(h, wg1q[...], preferred_element_type=F32))
    a_up = jnp.dot(at_ref[...], waq[...], preferred_element_type=F32)
    p_up = jnp.dot(po_ref[...], wpq[...], preferred_element_type=F32)
    o_ref[...] = (g0 * a_up + g1 * p_up).astype(o_ref.dtype)


def _merge(h, attn, pooled, w_gate, w_up_attn, w_up_pool):
    tn = 256
    tm = MERGE_ROWS
    goff = D_MODEL // tn
    return pl.pallas_call(
        _merge_kernel,
        grid=(D_MODEL // tn, M_ALL // tm),
        in_specs=[pl.BlockSpec((tm, D_MODEL), lambda j, i: (i, 0)),
                  pl.BlockSpec((tm, ATTN_WIDTH), lambda j, i: (i, 0)),
                  pl.BlockSpec((tm, POOL_WIDTH), lambda j, i: (i, 0)),
                  pl.BlockSpec((D_MODEL, tn), lambda j, i: (0, j)),
                  pl.BlockSpec((D_MODEL, tn), lambda j, i: (0, j + goff)),
                  pl.BlockSpec((ATTN_WIDTH, tn), lambda j, i: (0, j)),
                  pl.BlockSpec((POOL_WIDTH, tn), lambda j, i: (0, j))],
        out_specs=pl.BlockSpec((tm, tn), lambda j, i: (i, j)),
        out_shape=jax.ShapeDtypeStruct((M_ALL, D_MODEL), BF16),
        scratch_shapes=[pltpu.VMEM((D_MODEL, tn), BF16), pltpu.VMEM((D_MODEL, tn), BF16),
                        pltpu.VMEM((ATTN_WIDTH, tn), BF16), pltpu.VMEM((POOL_WIDTH, tn), BF16)],
        compiler_params=_params(("arbitrary", "arbitrary")),
    )(h, attn, pooled, w_gate, w_gate, w_up_attn, w_up_pool)


def _lambda(lq1, lk1, lq2, lk2):
    return (jnp.exp(jnp.sum(lq1[...] * lk1[...], axis=-1, keepdims=True))
            - jnp.exp(jnp.sum(lq2[...] * lk2[...], axis=-1, keepdims=True)) + LAM_INIT)


def _split_halves(q):
    lane = lax.broadcasted_iota(jnp.int32, q.shape, 1)
    zero = jnp.zeros_like(q)
    return jnp.concatenate([jnp.where(lane < HALF_DIM, q, zero),
                            jnp.where(lane >= HALF_DIM, q, zero)], axis=0)


def _head_out(o, lam, g):
    t = o.shape[0] // 2
    a = o[:t] - lam * o[t:]
    return _rms(a, g) * (1.0 - LAM_INIT)


def _prompt_attn_kernel(q_ref, k_ref, v_ref, lq1, lk1, lq2, lk2, g_ref, o_ref,
                        kb, vb, m_s, l_s, acc_s):
    qi = pl.program_id(2)
    tq = q_ref.shape[0]

    @pl.when(qi == 0)
    def _():
        kb[...] = k_ref[...].astype(BF16)
        vb[...] = v_ref[...].astype(BF16)

    q2 = _split_halves(q_ref[...])
    m_s[...] = jnp.full(m_s.shape, NEG, F32)
    l_s[...] = jnp.zeros(l_s.shape, F32)
    acc_s[...] = jnp.zeros(acc_s.shape, F32)

    def step(j, masked):
        start = pl.multiple_of(j * tq, tq)
        k = kb[pl.ds(start, tq), :]
        v = vb[pl.ds(start, tq), :]
        s = lax.dot_general(q2, k, (((1,), (1,)), ((), ())), preferred_element_type=F32)
        if masked:
            row = lax.broadcasted_iota(jnp.int32, s.shape, 0)
            col = lax.broadcasted_iota(jnp.int32, s.shape, 1)
            s = jnp.where(col <= row % tq, s, NEG)
        m_prev = m_s[...]
        m_new = jnp.maximum(m_prev, jnp.max(s, axis=-1, keepdims=True))
        p = jnp.exp(s - m_new)
        corr = jnp.exp(m_prev - m_new)
        l_s[...] = l_s[...] * corr + jnp.sum(p, axis=-1, keepdims=True)
        acc_s[...] = acc_s[...] * corr + jnp.dot(p.astype(BF16), v, preferred_element_type=F32)
        m_s[...] = m_new

    def body(j, carry):
        step(j, False)
        return carry

    lax.fori_loop(0, qi, body, 0)
    step(qi, True)

    o = acc_s[...] / l_s[...]
    lam = _lambda(lq1, lk1, lq2, lk2)
    o_ref[...] = _head_out(o, lam, g_ref[...]).astype(o_ref.dtype)


def _prompt_attn(q, k, v, lams, subln_g):
    tq = Q_TILE
    nq = SEQ // tq
    small = pl.BlockSpec((1, HALF_DIM), lambda b, h, i: (0, 0))
    return pl.pallas_call(
        _prompt_attn_kernel,
        grid=(BATCH, N_HEADS, nq),
        in_specs=[pl.BlockSpec((tq, HEAD_DIM), lambda b, h, i: (b * nq + i, h)),
                  pl.BlockSpec((SEQ, HEAD_DIM), lambda b, h, i: (b, h)),
                  pl.BlockSpec((SEQ, HEAD_DIM), lambda b, h, i: (b, h)),
                  small, small, small, small,
                  pl.BlockSpec((1, HEAD_DIM), lambda b, h, i: (0, 0))],
        out_specs=pl.BlockSpec((tq, HEAD_DIM), lambda b, h, i: (b * nq + i, h)),
        out_shape=jax.ShapeDtypeStruct((M_PROMPT, ATTN_WIDTH), BF16),
        scratch_shapes=[pltpu.VMEM((SEQ, HEAD_DIM), BF16), pltpu.VMEM((SEQ, HEAD_DIM), BF16),
                        pltpu.VMEM((2 * tq, 1), F32), pltpu.VMEM((2 * tq, 1), F32),
                        pltpu.VMEM((2 * tq, HEAD_DIM), F32)],
        compiler_params=_params(("arbitrary", "arbitrary", "arbitrary")),
    )(q, k, v, *lams, subln_g.reshape(1, HEAD_DIM))


_GROUP_ROWS = HEADS_PER_GROUP * 2 * DEC_SEQ
_SLAB_ROWS = PAGE_SIZE * HEADS_PER_GROUP
_N_GROUPS = N_HEADS // HEADS_PER_GROUP


def _online_update(m_ref, l_ref, acc_ref, idx, s, v):
    m_prev = m_ref[idx]
    m_new = jnp.maximum(m_prev, jnp.max(s, axis=-1, keepdims=True))
    p = jnp.exp(s - m_new)
    corr = jnp.exp(m_prev - m_new)
    l_ref[idx] = l_ref[idx] * corr + jnp.sum(p, axis=-1, keepdims=True)
    acc_ref[idx] = acc_ref[idx] * corr + jnp.dot(p.astype(v.dtype), v, preferred_element_type=F32)
    m_ref[idx] = m_new


def _sample_attn_kernel(pt_ref, q_ref, kn_ref, vn_ref, *rest):
    del pt_ref
    kpages = rest[:PAGES_PER_STEP]
    vpages = rest[PAGES_PER_STEP:2 * PAGES_PER_STEP]
    lq1, lk1, lq2, lk2, g_ref, o_ref, qz_s, bias_s, m_s, l_s, acc_s = rest[2 * PAGES_PER_STEP:]
    step = pl.program_id(1)

    @pl.when(step == 0)
    def _():
        for h in range(N_HEADS):
            grp, hl = divmod(h, HEADS_PER_GROUP)
            qh = q_ref[:, h * HEAD_DIM:(h + 1) * HEAD_DIM]
            qz_s[grp, hl * 2 * DEC_SEQ:(hl + 1) * 2 * DEC_SEQ, :] = _split_halves(qh)
        row = lax.broadcasted_iota(jnp.int32, bias_s.shape, 0)
        col = lax.broadcasted_iota(jnp.int32, bias_s.shape, 1)
        bias_s[...] = jnp.where(col % HEADS_PER_GROUP == row // (2 * DEC_SEQ), 0.0, NEG)
        m_s[...] = jnp.full(m_s.shape, NEG, F32)
        l_s[...] = jnp.zeros(l_s.shape, F32)
        acc_s[...] = jnp.zeros(acc_s.shape, F32)

    for grp in range(_N_GROUPS):
        lo = grp * HEADS_PER_GROUP
        qz = qz_s[grp].astype(BF16)
        k2 = jnp.concatenate(
            [kp[0, :, lo:lo + HEADS_PER_GROUP, :].reshape(_SLAB_ROWS, HEAD_DIM).astype(BF16)
             for kp in kpages], axis=0)
        v2 = jnp.concatenate(
            [vp[0, :, lo:lo + HEADS_PER_GROUP, :].reshape(_SLAB_ROWS, HEAD_DIM).astype(BF16)
             for vp in vpages], axis=0)
        s = lax.dot_general(qz, k2, (((1,), (1,)), ((), ())), preferred_element_type=F32)
        s = s + bias_s[...]
        _online_update(m_s, l_s, acc_s, grp, s, v2)

    @pl.when(step == pl.num_programs(1) - 1)
    def _():
        lam = _lambda(lq1, lk1, lq2, lk2)
        g = g_ref[...]
        for h in range(N_HEADS):
            grp, hl = divmod(h, HEADS_PER_GROUP)
            rows = pl.ds(hl * 2 * DEC_SEQ, 2 * DEC_SEQ)
            cols = slice(h * HEAD_DIM, (h + 1) * HEAD_DIM)
            qh = qz_s[grp, rows, :]
            s = lax.dot_general(qh, kn_ref[:, cols], (((1,), (1,)), ((), ())),
                                preferred_element_type=F32)
            row = lax.broadcasted_iota(jnp.int32, s.shape, 0)
            col = lax.broadcasted_iota(jnp.int32, s.shape, 1)
            causal = col <= row % DEC_SEQ
            s = jnp.where(causal, s, NEG)
            m_prev = m_s[grp, rows, :]
            m_new = jnp.maximum(m_prev, jnp.max(s, axis=-1, keepdims=True))
            p = jnp.where(causal, jnp.exp(s - m_new), 0.0)
            corr = jnp.exp(m_prev - m_new)
            l = l_s[grp, rows, :] * corr + jnp.sum(p, axis=-1, keepdims=True)
            acc = acc_s[grp, rows, :] * corr + jnp.dot(p, vn_ref[:, cols],
                                                       preferred_element_type=F32)
            o_ref[:, cols] = _head_out(acc / l, lam, g).astype(o_ref.dtype)


def _sample_attn(q, k, v, cache_k, cache_v, page_table, lams, subln_g):
    pps = PAGES_PER_STEP

    def new_spec():
        return pl.BlockSpec((DEC_SEQ, ATTN_WIDTH), lambda b, s, pt: (b, 0))

    def page_spec(p):
        return pl.BlockSpec((1, PAGE_SIZE, N_HEADS, HEAD_DIM),
                            lambda b, s, pt: (pt[b, s * pps + p], 0, 0, 0))

    small = pl.BlockSpec((1, HALF_DIM), lambda b, s, pt: (0, 0))
    grid_spec = pltpu.PrefetchScalarGridSpec(
        num_scalar_prefetch=1,
        grid=(DEC_BATCH, N_PAGES // pps),
        in_specs=[new_spec(), new_spec(), new_spec()]
                 + [page_spec(p) for p in range(pps)] + [page_spec(p) for p in range(pps)]
                 + [small, small, small, small,
                    pl.BlockSpec((1, HEAD_DIM), lambda b, s, pt: (0, 0))],
        out_specs=pl.BlockSpec((DEC_SEQ, ATTN_WIDTH), lambda b, s, pt: (b, 0)),
        scratch_shapes=[pltpu.VMEM((_N_GROUPS, _GROUP_ROWS, HEAD_DIM), F32),
                        pltpu.VMEM((_GROUP_ROWS, pps * _SLAB_ROWS), F32),
                        pltpu.VMEM((_N_GROUPS, _GROUP_ROWS, 1), F32),
                        pltpu.VMEM((_N_GROUPS, _GROUP_ROWS, 1), F32),
                        pltpu.VMEM((_N_GROUPS, _GROUP_ROWS, HEAD_DIM), F32)],
    )
    return pl.pallas_call(
        _sample_attn_kernel,
        grid_spec=grid_spec,
        out_shape=jax.ShapeDtypeStruct((M_SAMPLE, ATTN_WIDTH), F32),
        compiler_params=_params(("arbitrary", "arbitrary")),
    )(page_table, q, k, v, *([cache_k] * pps), *([cache_v] * pps), *lams,
      subln_g.reshape(1, HEAD_DIM))


_HALO = 16


def _pool_groups(ext_ref, lead, t, pos, wq_ref, scale_ref, write):
    for g, w in enumerate(POOL_WINDOWS):
        cols = slice(g * POOL_GROUP, (g + 1) * POOL_GROUP)
        tot = None
        for s in range(w):
            part = ext_ref[lead + (pl.ds(_HALO - s, t), cols)]
            tot = part if tot is None else tot + part
        cnt = jnp.minimum(pos + 1, w).astype(F32)
        pooled = tot / cnt - ext_ref[lead + (pl.ds(_HALO, t), cols)]
        write(g, cols, pooled)


def _pool_prompt_kernel(u_ref, w_ref, scale_ref, o_ref, ext_s, wq_s):
    i = pl.program_id(1)
    t = u_ref.shape[0]

    @pl.when((pl.program_id(0) == 0) & (i == 0))
    def _():
        wq_s[...] = w_ref[...].astype(BF16)

    @pl.when(i == 0)
    def _():
        ext_s[0:_HALO, :] = jnp.zeros((_HALO, POOL_WIDTH), F32)

    @pl.when(i != 0)
    def _():
        ext_s[0:_HALO, :] = ext_s[t:t + _HALO, :]

    ext_s[_HALO:_HALO + t, :] = u_ref[...]
    pos = i * t + lax.broadcasted_iota(jnp.int32, (t, 1), 0)

    def write(g, cols, pooled):
        y = jnp.dot(pooled.astype(BF16), wq_s[g], preferred_element_type=F32)
        o_ref[:, cols] = (y * scale_ref[:, cols]).astype(o_ref.dtype)

    _pool_groups(ext_s, (), t, pos, wq_s, scale_ref, write)


def _pool_prompt(u, w_pool_mix, pool_scale):
    t = 256
    nt = SEQ // t
    return pl.pallas_call(
        _pool_prompt_kernel,
        grid=(BATCH, nt),
        in_specs=[pl.BlockSpec((t, POOL_WIDTH), lambda b, i: (b * nt + i, 0)),
                  pl.BlockSpec((len(POOL_WINDOWS), POOL_GROUP, POOL_GROUP), lambda b, i: (0, 0, 0)),
                  pl.BlockSpec((1, POOL_WIDTH), lambda b, i: (0, 0))],
        out_specs=pl.BlockSpec((t, POOL_WIDTH), lambda b, i: (b * nt + i, 0)),
        out_shape=jax.ShapeDtypeStruct((M_PROMPT, POOL_WIDTH), BF16),
        scratch_shapes=[pltpu.VMEM((_HALO + t, POOL_WIDTH), F32),
                        pltpu.VMEM((len(POOL_WINDOWS), POOL_GROUP, POOL_GROUP), BF16)],
        compiler_params=_params(("arbitrary", "arbitrary")),
    )(u, w_pool_mix, pool_scale.reshape(1, POOL_WIDTH))


def _pool_sample_kernel(state_ref, u_ref, w_ref, scale_ref, o_ref, ext_s):
    ext_s[:, 0:_HALO, :] = state_ref[...]
    ext_s[:, _HALO:_HALO + DEC_SEQ, :] = u_ref[...]
    pos = PAST_LEN + lax.broadcasted_iota(jnp.int32, (1, DEC_SEQ, 1), 1)

    def write(g, cols, pooled):
        x = pooled.reshape(M_SAMPLE, POOL_GROUP).astype(BF16)
        y = jnp.dot(x, w_ref[g].astype(BF16), preferred_element_type=F32)
        o_ref[:, cols] = (y * scale_ref[:, cols]).astype(o_ref.dtype)

    _pool_groups(ext_s, (slice(None),), DEC_SEQ, pos, w_ref, scale_ref, write)


def _pool_sample(state16, u3, w_pool_mix, pool_scale):
    return pl.pallas_call(
        _pool_sample_kernel,
        out_shape=jax.ShapeDtypeStruct((M_SAMPLE, POOL_WIDTH), BF16),
        scratch_shapes=[pltpu.VMEM((DEC_BATCH, _HALO + DEC_SEQ, POOL_WIDTH), F32)],
        compiler_params=pltpu.CompilerParams(vmem_limit_bytes=VMEM_LIMIT_BYTES),
    )(state16, u3, w_pool_mix, pool_scale.reshape(1, POOL_WIDTH))


def _rope_tables():
    inv = 1.0 / (ROPE_THETA ** (jnp.arange(0, HALF_DIM, 2, dtype=F32) / HALF_DIM))
    pos = jnp.concatenate([jnp.tile(jnp.arange(SEQ), BATCH),
                           jnp.tile(PAST_LEN + jnp.arange(DEC_SEQ), DEC_BATCH)])
    ang = pos.astype(F32)[:, None] * inv[None, :]
    cos, sin = jnp.cos(ang), jnp.sin(ang)
    cos = jnp.tile(cos, (1, HEAD_DIM // (HALF_DIM // 2)))
    sin = jnp.tile(jnp.concatenate([-sin, sin], axis=1), (1, 2))
    return cos, sin


def _mod_rows(ada):
    ada = ada[:BATCH + DEC_BATCH].reshape(BATCH + DEC_BATCH, N_ADA, D_MODEL)
    prompt = jnp.broadcast_to(ada[:BATCH].transpose(1, 0, 2)[:, :, None, :],
                              (N_ADA, BATCH, ROW_TILE, D_MODEL))
    sample = jnp.repeat(ada[BATCH:], DEC_SEQ, axis=0)
    sample = sample.reshape(M_SAMPLE // ROW_TILE, ROW_TILE, N_ADA, D_MODEL).transpose(2, 0, 1, 3)
    return jnp.concatenate([prompt, sample], axis=1)


def kernel(x_prompt, x_sample, cache_k, cache_v, state_pool, page_table, c_prompt, c_sample,
           w_ada, b_ada, g_pre_ffn1, w1_ffn1, w3_ffn1, w2_ffn1, g_post_ffn1,
           g_pre_mix, w_in, lambda_q1, lambda_k1, lambda_q2, lambda_k2, subln_g,
           w_pool_mix, pool_scale, w_up_attn, w_up_pool, w_merge_gate, w_out, g_post_mix,
           g_pre_ffn2, w1_ffn2, w3_ffn2, w2_ffn2, g_post_ffn2):
    x = jnp.concatenate([x_prompt.reshape(M_PROMPT, D_MODEL), x_sample.reshape(M_SAMPLE, D_MODEL)])
    c = jnp.concatenate([c_prompt, c_sample])
    c = jnp.pad(c, ((0, 48 - c.shape[0]), (0, 0)))
    mod = _mod_rows(_ada(c, w_ada, b_ada))
    lams = [v.reshape(1, HALF_DIM) for v in (lambda_q1, lambda_k1, lambda_q2, lambda_k2)]

    h = _norm_mod(x, g_pre_ffn1, mod, 0)
    f = _down(_swiglu_up(h, w1_ffn1, w3_ffn1), w2_ffn1)
    x, h = _resid_norm_mod(x, f, g_post_ffn1, g_pre_mix, mod, 0, 1, 0.5)

    tabs = _rope_tables()
    q = _proj(h, w_in, 0, ATTN_WIDTH, BF16, tabs, HALF_DIM ** -0.5)
    k = _proj(h, w_in, ATTN_WIDTH, ATTN_WIDTH, F32, tabs)
    v = _proj(h, w_in, 2 * ATTN_WIDTH, ATTN_WIDTH, F32)
    u = _proj(h, w_in, 3 * ATTN_WIDTH, POOL_WIDTH, F32)

    attn_p = _prompt_attn(q, k, v, lams, subln_g)
    attn_s = _sample_attn(q[M_PROMPT:].astype(F32), k[M_PROMPT:], v[M_PROMPT:],
                          cache_k, cache_v, page_table, lams, subln_g)
    attn = jnp.concatenate([attn_p, attn_s.astype(BF16)])

    u_sample = u[M_PROMPT:].reshape(DEC_BATCH, DEC_SEQ, POOL_WIDTH)
    state16 = jnp.pad(state_pool, ((0, 0), (_HALO - POOL_BUF, 0), (0, 0)))
    pooled = jnp.concatenate([_pool_prompt(u, w_pool_mix, pool_scale),
                              _pool_sample(state16, u_sample, w_pool_mix, pool_scale)])

    merged = _merge(h, attn, pooled, w_merge_gate, w_up_attn, w_up_pool)
    m = _proj(merged, w_out, 0, D_MODEL, F32)
    x, h = _resid_norm_mod(x, m, g_post_mix, g_pre_ffn2, mod, 1, 2, 1.0)

    f = _down(_swiglu_up(h, w1_ffn2, w3_ffn2), w2_ffn2)
    y = _resid(x, f, g_post_ffn2, mod, 2, 0.5)

    y_prompt = y[:M_PROMPT].reshape(BATCH, SEQ, D_MODEL)
    y_sample = y[M_PROMPT:].reshape(DEC_BATCH, DEC_SEQ, D_MODEL)
    k_prompt = k[:M_PROMPT].reshape(BATCH, SEQ, N_HEADS, HEAD_DIM)
    v_prompt = v[:M_PROMPT].reshape(BATCH, SEQ, N_HEADS, HEAD_DIM)
    k_sample = k[M_PROMPT:].reshape(DEC_BATCH, DEC_SEQ, N_HEADS, HEAD_DIM)
    v_sample = v[M_PROMPT:].reshape(DEC_BATCH, DEC_SEQ, N_HEADS, HEAD_DIM)
    pool_prompt = u[:M_PROMPT].reshape(BATCH, SEQ, POOL_WIDTH)[:, SEQ - POOL_BUF:]
    pool_sample = jnp.concatenate([state_pool, u_sample], axis=1)[:, -POOL_BUF:]
    return (y_prompt, y_sample, k_prompt, v_prompt, pool_prompt, k_sample, v_sample, pool_sample)
```

```python
import functools
import math

import jax
import jax.numpy as jnp
from jax import lax
from jax.experimental import pallas as pl
from jax.experimental.pallas import tpu as pltpu

F32 = jnp.float32
BF16 = jnp.bfloat16

D_MODEL = 4096
BATCH = 2
SEQ = 4096
DEC_BATCH = 32
DEC_SEQ = 8
PAST_LEN = 8192
PAGE_SIZE = 128
N_PAGES = PAST_LEN // PAGE_SIZE
ATTN_WIDTH = D_MODEL // 2
HALF_DIM = 64
HEAD_DIM = 2 * HALF_DIM
N_HEADS = ATTN_WIDTH // HEAD_DIM
POOL_WIDTH = D_MODEL - ATTN_WIDTH
POOL_WINDOWS = (2, 4, 8, 16)
POOL_GROUP = POOL_WIDTH // len(POOL_WINDOWS)
POOL_BUF = max(POOL_WINDOWS) - 1
D_FF = ((8 * D_MODEL // 3 + 255) // 256) * 256
ROPE_THETA = 10000.0
N_ADA = 9
EPS = 1e-6
NEG = -1e30
LAM_INIT = 0.8 - 0.6 * math.exp(-0.3 * 0)

M_PROMPT = BATCH * SEQ
M_SAMPLE = DEC_BATCH * DEC_SEQ
M_ALL = M_PROMPT + M_SAMPLE

VMEM_LIMIT_BYTES = 58 * 1024 * 1024

ROW_TILE = 128
MM_ROWS = 768
UP_ROWS = 1408
LOG2_E = math.log2(math.e)
MERGE_ROWS = 384
PAGES_PER_STEP = 4
HEADS_PER_GROUP = 8
Q_TILE = 512


def _params(sem):
    return pltpu.CompilerParams(dimension_semantics=sem, vmem_limit_bytes=VMEM_LIMIT_BYTES)


def _sigmoid(x):
    return 1.0 / (1.0 + jnp.exp(-x))


def _rms(x, g):
    return x * lax.rsqrt(jnp.mean(x * x, axis=-1, keepdims=True) + EPS) * g


def _ada_kernel(c_ref, w_ref, b_ref, o_ref):
    c = c_ref[...]
    a = (c * _sigmoid(c)).astype(BF16)
    o_ref[...] = jnp.dot(a, w_ref[...].astype(BF16), preferred_element_type=F32) + b_ref[...]


def _ada(c_pad, w_ada, b_ada):
    rows = c_pad.shape[0]
    n = w_ada.shape[1]
    tn = 1024
    return pl.pallas_call(
        _ada_kernel,
        grid=(n // tn,),
        in_specs=[pl.BlockSpec((rows, D_MODEL), lambda j: (0, 0)),
                  pl.BlockSpec((D_MODEL, tn), lambda j: (0, j)),
                  pl.BlockSpec((1, tn), lambda j: (0, j))],
        out_specs=pl.BlockSpec((rows, tn), lambda j: (0, j)),
        out_shape=jax.ShapeDtypeStruct((rows, n), F32),
        compiler_params=_params(("arbitrary",)),
    )(c_pad, w_ada, b_ada.reshape(1, n))


_PROMPT_TILES = M_PROMPT // ROW_TILE
_TILES_PER_BATCH = SEQ // ROW_TILE


def _mod_group(i):
    return jnp.where(i < _PROMPT_TILES, i // _TILES_PER_BATCH, i - _PROMPT_TILES + BATCH)


def _mod_spec(j):
    return pl.BlockSpec((1, 1, ROW_TILE, D_MODEL), lambda i: (j, _mod_group(i), 0, 0))


def _row_spec():
    return pl.BlockSpec((ROW_TILE, D_MODEL), lambda i: (i, 0))


def _pair_specs():
    return [pl.BlockSpec((ROW_TILE, D_MODEL), lambda i: (jnp.minimum(i, _PROMPT_TILES - 1), 0)),
            pl.BlockSpec((ROW_TILE, D_MODEL), lambda i: (jnp.maximum(i - _PROMPT_TILES, 0), 0))]


def _pair_shapes():
    return [jax.ShapeDtypeStruct((M_PROMPT, D_MODEL), F32),
            jax.ShapeDtypeStruct((M_SAMPLE, D_MODEL), F32)]


def _vec_spec():
    return pl.BlockSpec((1, D_MODEL), lambda i: (0, 0))


def _read_pair(xp_ref, xs_ref):
    return jnp.where(pl.program_id(0) < _PROMPT_TILES, xp_ref[...], xs_ref[...])


def _write_pair(xp_ref, xs_ref, x):
    @pl.when(pl.program_id(0) < _PROMPT_TILES)
    def _():
        xp_ref[...] = x

    @pl.when(pl.program_id(0) >= _PROMPT_TILES)
    def _():
        xs_ref[...] = x


def _norm_mod_kernel(xp_ref, xs_ref, g_ref, scale_ref, shift_ref, h_ref):
    h = _rms(_read_pair(xp_ref, xs_ref), g_ref[...])
    h_ref[...] = (h * (1.0 + scale_ref[0, 0]) + shift_ref[0, 0]).astype(h_ref.dtype)


def _norm_mod(x, g, mod, j):
    return pl.pallas_call(
        _norm_mod_kernel,
        grid=(M_ALL // ROW_TILE,),
        in_specs=_pair_specs() + [_vec_spec(), _mod_spec(3 * j + 1), _mod_spec(3 * j)],
        out_specs=_row_spec(),
        out_shape=jax.ShapeDtypeStruct((M_ALL, D_MODEL), BF16),
        compiler_params=_params(("arbitrary",)),
    )(*x, g.reshape(1, D_MODEL), mod, mod)


def _resid_norm_mod_kernel(xp_ref, xs_ref, f_ref, gpost_ref, gate_ref, gpre_ref, scale_ref,
                           shift_ref, xpo_ref, xso_ref, h_ref, *, coef):
    x = _read_pair(xp_ref, xs_ref) + coef * gate_ref[0, 0] * _rms(f_ref[...], gpost_ref[...])
    _write_pair(xpo_ref, xso_ref, x)
    h = _rms(x, gpre_ref[...])
    h_ref[...] = (h * (1.0 + scale_ref[0, 0]) + shift_ref[0, 0]).astype(h_ref.dtype)


def _resid_norm_mod(x, f, gpost, gpre, mod, j_prev, j_next, coef):
    xp, xs, h = pl.pallas_call(
        functools.partial(_resid_norm_mod_kernel, coef=coef),
        grid=(M_ALL // ROW_TILE,),
        in_specs=_pair_specs() + [_row_spec(), _vec_spec(), _mod_spec(3 * j_prev + 2), _vec_spec(),
                                  _mod_spec(3 * j_next + 1), _mod_spec(3 * j_next)],
        out_specs=_pair_specs() + [_row_spec()],
        out_shape=_pair_shapes() + [jax.ShapeDtypeStruct((M_ALL, D_MODEL), BF16)],
        compiler_params=_params(("arbitrary",)),
    )(*x, f, gpost.reshape(1, D_MODEL), mod, gpre.reshape(1, D_MODEL), mod, mod)
    return (xp, xs), h


def _resid_kernel(xp_ref, xs_ref, f_ref, gpost_ref, gate_ref, xpo_ref, xso_ref, *, coef):
    x = _read_pair(xp_ref, xs_ref) + coef * gate_ref[0, 0] * _rms(f_ref[...], gpost_ref[...])
    _write_pair(xpo_ref, xso_ref, x)


def _resid(x, f, gpost, mod, j_prev, coef):
    return pl.pallas_call(
        functools.partial(_resid_kernel, coef=coef),
        grid=(M_ALL // ROW_TILE,),
        in_specs=_pair_specs() + [_row_spec(), _vec_spec(), _mod_spec(3 * j_prev + 2)],
        out_specs=_pair_specs(),
        out_shape=_pair_shapes(),
        compiler_params=_params(("arbitrary",)),
    )(*x, f, gpost.reshape(1, D_MODEL), mod)


def _swiglu_up_kernel(a_ref, w1_ref, w3_ref, o_ref, w1q, w3q):
    @pl.when(pl.program_id(1) == 0)
    def _():
        w1q[...] = w1_ref[...].astype(BF16)
        w3q[...] = w3_ref[...].astype(BF16)

    a = a_ref[...]
    u = jnp.dot(a, w1q[...], preferred_element_type=F32)
    v = jnp.dot(a, w3q[...], preferred_element_type=F32)
    o_ref[...] = (u * _sigmoid(u) * v).astype(o_ref.dtype)


def _swiglu_up(h, w1, w3):
    tn = 256
    tm = UP_ROWS
    return pl.pallas_call(
        _swiglu_up_kernel,
        grid=(D_FF // tn, M_ALL // tm),
        in_specs=[pl.BlockSpec((tm, D_MODEL), lambda j, i: (i, 0)),
                  pl.BlockSpec((D_MODEL, tn), lambda j, i: (0, j)),
                  pl.BlockSpec((D_MODEL, tn), lambda j, i: (0, j))],
        out_specs=pl.BlockSpec((tm, tn), lambda j, i: (i, j)),
        out_shape=jax.ShapeDtypeStruct((M_ALL, D_FF), BF16),
        scratch_shapes=[pltpu.VMEM((D_MODEL, tn), BF16), pltpu.VMEM((D_MODEL, tn), BF16)],
        compiler_params=_params(("arbitrary", "arbitrary")),
    )(h, w1, w3)


def _down_kernel(a0_ref, a1_ref, w0_ref, w1_ref, o_ref):
    k = pl.program_id(1)
    last = pl.num_programs(1) - 1

    def first():
        return jnp.dot(a0_ref[...], w0_ref[...].astype(BF16), preferred_element_type=F32)

    def second():
        return jnp.dot(a1_ref[...], w1_ref[...].astype(BF16), preferred_element_type=F32)

    @pl.when(k == 0)
    def _():
        o_ref[...] = first() + second()

    @pl.when((k != 0) & (k != last))
    def _():
        o_ref[...] += first() + second()

    @pl.when(k == last)
    def _():
        o_ref[...] += first()


def _down(g, w2):
    tk = 256
    tm = MM_ROWS
    nk = D_FF // tk
    assert nk % 2 == 1 and nk > 2
    return pl.pallas_call(
        _down_kernel,
        grid=(M_ALL // tm, (nk + 1) // 2),
        in_specs=[pl.BlockSpec((tm, tk), lambda i, k: (i, 2 * k)),
                  pl.BlockSpec((tm, tk), lambda i, k: (i, jnp.minimum(2 * k + 1, nk - 1))),
                  pl.BlockSpec((tk, D_MODEL), lambda i, k: (2 * k, 0)),
                  pl.BlockSpec((tk, D_MODEL), lambda i, k: (jnp.minimum(2 * k + 1, nk - 1), 0))],
        out_specs=pl.BlockSpec((tm, D_MODEL), lambda i, k: (i, 0)),
        out_shape=jax.ShapeDtypeStruct((M_ALL, D_MODEL), F32),
        compiler_params=_params(("arbitrary", "arbitrary")),
    )(g, g, w2, w2)


def _proj_kernel(a_ref, w_ref, *rest, rope, out_scale):
    if rope:
        cos_ref, sin_ref, o_ref, wq = rest
    else:
        o_ref, wq = rest

    @pl.when(pl.program_id(1) == 0)
    def _():
        wq[...] = w_ref[...].astype(BF16)

    y = jnp.dot(a_ref[...], wq[...], preferred_element_type=F32)
    if rope:
        tn = y.shape[1]
        reps = tn // HEAD_DIM
        cos = jnp.tile(cos_ref[...], (1, reps))
        sin = jnp.tile(sin_ref[...], (1, reps))
        lane = lax.broadcasted_iota(jnp.int32, y.shape, 1)
        first = (lane % HALF_DIM) < (HALF_DIM // 2)
        partner = jnp.where(first, pltpu.roll(y, tn - HALF_DIM // 2, 1), pltpu.roll(y, HALF_DIM // 2, 1))
        y = y * cos + partner * sin
    if out_scale != 1.0:
        y = y * out_scale
    o_ref[...] = y.astype(o_ref.dtype)


def _proj(a, w, col0, n, out_dtype, rope_tabs=None, out_scale=1.0):
    k = a.shape[1]
    tn = 512
    tm = MM_ROWS
    joff = col0 // tn
    in_specs = [pl.BlockSpec((tm, k), lambda j, i: (i, 0)),
                pl.BlockSpec((k, tn), lambda j, i: (0, j + joff))]
    args = [a, w]
    if rope_tabs is not None:
        in_specs += [pl.BlockSpec((tm, HEAD_DIM), lambda j, i: (i, 0))] * 2
        args += list(rope_tabs)
    return pl.pallas_call(
        functools.partial(_proj_kernel, rope=rope_tabs is not None, out_scale=out_scale),
        grid=(n // tn, M_ALL // tm),
        in_specs=in_specs,
        out_specs=pl.BlockSpec((tm, tn), lambda j, i: (i, j)),
        out_shape=jax.ShapeDtypeStruct((M_ALL, n), out_dtype),
        scratch_shapes=[pltpu.VMEM((k, tn), BF16)],
        compiler_params=_params(("arbitrary", "arbitrary")),
    )(*args)


def _merge_kernel(h_ref, at_ref, po_ref, wg0_ref, wg1_ref, wa_ref, wp_ref, o_ref,
                  wg0q, wg1q, waq, wpq):
    @pl.when(pl.program_id(1) == 0)
    def _():
        wg0q[...] = wg0_ref[...].astype(BF16)
        wg1q[...] = wg1_ref[...].astype(BF16)
        waq[...] = wa_ref[...].astype(BF16)
        wpq[...] = wp_ref[...].astype(BF16)

    h = h_ref[...]
    g0 = _sigmoid(jnp.dot(h, wg0q[...], preferred_element_type=F32))
    g1 = _sigmoid(jnp.dot(h, wg1q[...], preferred_element_type=F32))
    a_up = jnp.dot(at_ref[...], waq[...], preferred_element_type=F32)
    p_up = jnp.dot(po_ref[...], wpq[...], preferred_element_type=F32)
    o_ref[...] = (g0 * a_up + g1 * p_up).astype(o_ref.dtype)


def _merge(h, attn, pooled, w_gate, w_up_attn, w_up_pool):
    tn = 256
    tm = MERGE_ROWS
    goff = D_MODEL // tn
    return pl.pallas_call(
        _merge_kernel,
        grid=(D_MODEL // tn, M_ALL // tm),
        in_specs=[pl.BlockSpec((tm, D_MODEL), lambda j, i: (i, 0)),
                  pl.BlockSpec((tm, ATTN_WIDTH), lambda j, i: (i, 0)),
                  pl.BlockSpec((tm, POOL_WIDTH), lambda j, i: (i, 0)),
                  pl.BlockSpec((D_MODEL, tn), lambda j, i: (0, j)),
                  pl.BlockSpec((D_MODEL, tn), lambda j, i: (0, j + goff)),
                  pl.BlockSpec((ATTN_WIDTH, tn), lambda j, i: (0, j)),
                  pl.BlockSpec((POOL_WIDTH, tn), lambda j, i: (0, j))],
        out_specs=pl.BlockSpec((tm, tn), lambda j, i: (i, j)),
        out_shape=jax.ShapeDtypeStruct((M_ALL, D_MODEL), BF16),
        scratch_shapes=[pltpu.VMEM((D_MODEL, tn), BF16), pltpu.VMEM((D_MODEL, tn), BF16),
                        pltpu.VMEM((ATTN_WIDTH, tn), BF16), pltpu.VMEM((POOL_WIDTH, tn), BF16)],
        compiler_params=_params(("arbitrary", "arbitrary")),
    )(h, attn, pooled, w_gate, w_gate, w_up_attn, w_up_pool)


def _lambda(lq1, lk1, lq2, lk2):
    return (jnp.exp(jnp.sum(lq1[...] * lk1[...], axis=-1, keepdims=True))
            - jnp.exp(jnp.sum(lq2[...] * lk2[...], axis=-1, keepdims=True)) + LAM_INIT)


def _split_halves(q):
    lane = lax.broadcasted_iota(jnp.int32, q.shape, 1)
    zero = jnp.zeros_like(q)
    return jnp.concatenate([jnp.where(lane < HALF_DIM, q, zero),
                            jnp.where(lane >= HALF_DIM, q, zero)], axis=0)


def _head_out(o, lam, g):
    t = o.shape[0] // 2
    a = o[:t] - lam * o[t:]
    return _rms(a, g) * (1.0 - LAM_INIT)


def _online_update(m_ref, acc_ref, idx, s, v_ones):
    m_prev = m_ref[idx]
    m_new = jnp.maximum(m_prev, jnp.max(s, axis=-1, keepdims=True))
    p = jnp.exp2(s - jnp.tile(m_new, (1, s.shape[1] // HEAD_DIM)))
    alpha = jnp.exp2(m_prev - m_new)
    acc_ref[idx] = (acc_ref[idx] * jnp.tile(alpha, (1, 2))
                    + jnp.dot(p.astype(BF16), v_ones, preferred_element_type=F32))
    m_ref[idx] = m_new


def _prompt_attn_kernel(q_ref, k_ref, v_ref, lq1, lk1, lq2, lk2, g_ref, o_ref,
                        kb, vb, m_s, acc_s):
    qi = pl.program_id(2)
    tq = q_ref.shape[0]

    @pl.when(qi == 0)
    def _():
        kb[...] = k_ref[...].astype(BF16)
        vb[:, :HEAD_DIM] = v_ref[...].astype(BF16)
        vb[:, HEAD_DIM:] = jnp.ones((SEQ, HEAD_DIM), BF16)

    q2 = _split_halves(q_ref[...])
    m_s[...] = jnp.full(m_s.shape, NEG, F32)
    acc_s[...] = jnp.zeros(acc_s.shape, F32)

    def step(j, masked):
        start = pl.multiple_of(j * tq, tq)
        k = kb[pl.ds(start, tq), :]
        v = vb[pl.ds(start, tq), :]
        s = lax.dot_general(q2, k, (((1,), (1,)), ((), ())), preferred_element_type=F32)
        if masked:
            row = lax.broadcasted_iota(jnp.int32, s.shape, 0)
            col = lax.broadcasted_iota(jnp.int32, s.shape, 1)
            s = jnp.where(col <= row % tq, s, NEG)
        _online_update(m_s, acc_s, (Ellipsis,), s, v)

    def body(j, carry):
        step(j, False)
        return carry

    lax.fori_loop(0, qi, body, 0)
    step(qi, True)

    acc = acc_s[...]
    o = acc[:, :HEAD_DIM] / acc[:, HEAD_DIM:]
    lam = _lambda(lq1, lk1, lq2, lk2)
    o_ref[...] = _head_out(o, lam, g_ref[...]).astype(o_ref.dtype)


def _prompt_attn(q, k, v, lams, subln_g):
    tq = Q_TILE
    nq = SEQ // tq
    small = pl.BlockSpec((1, HALF_DIM), lambda b, h, i: (0, 0))
    return pl.pallas_call(
        _prompt_attn_kernel,
        grid=(BATCH, N_HEADS, nq),
        in_specs=[pl.BlockSpec((tq, HEAD_DIM), lambda b, h, i: (b * nq + i, h)),
                  pl.BlockSpec((SEQ, HEAD_DIM), lambda b, h, i: (b, h)),
                  pl.BlockSpec((SEQ, HEAD_DIM), lambda b, h, i: (b, h)),
                  small, small, small, small,
                  pl.BlockSpec((1, HEAD_DIM), lambda b, h, i: (0, 0))],
        out_specs=pl.BlockSpec((tq, HEAD_DIM), lambda b, h, i: (b * nq + i, h)),
        out_shape=jax.ShapeDtypeStruct((M_ALL, ATTN_WIDTH), BF16),
        scratch_shapes=[pltpu.VMEM((SEQ, HEAD_DIM), BF16), pltpu.VMEM((SEQ, 2 * HEAD_DIM), BF16),
                        pltpu.VMEM((2 * tq, HEAD_DIM), F32),
                        pltpu.VMEM((2 * tq, 2 * HEAD_DIM), F32)],
        compiler_params=_params(("arbitrary", "arbitrary", "arbitrary")),
    )(q, k, v, *lams, subln_g.reshape(1, HEAD_DIM))


_GROUP_ROWS = HEADS_PER_GROUP * 2 * DEC_SEQ
_SLAB_ROWS = PAGE_SIZE * HEADS_PER_GROUP
_N_GROUPS = N_HEADS // HEADS_PER_GROUP


def _sample_attn_kernel(pt_ref, q_ref, kn_ref, vn_ref, *rest):
    del pt_ref
    kpages = rest[:PAGES_PER_STEP]
    vpages = rest[PAGES_PER_STEP:2 * PAGES_PER_STEP]
    (lq1, lk1, lq2, lk2, g_ref, attn_in, o_ref,
     qz_s, bias_s, m_s, acc_s, out_s) = rest[2 * PAGES_PER_STEP:]
    del attn_in
    b = pl.program_id(0)
    step = pl.program_id(1)
    last_step = pl.num_programs(1) - 1

    @pl.when(step == 0)
    def _():
        for h in range(N_HEADS):
            grp, hl = divmod(h, HEADS_PER_GROUP)
            qh = q_ref[:, h * HEAD_DIM:(h + 1) * HEAD_DIM]
            qz_s[grp, hl * 2 * DEC_SEQ:(hl + 1) * 2 * DEC_SEQ, :] = _split_halves(qh)
        m_s[...] = jnp.full(m_s.shape, NEG, F32)
        acc_s[...] = jnp.zeros(acc_s.shape, F32)

    @pl.when((b == 0) & (step == 0))
    def _():
        row = lax.broadcasted_iota(jnp.int32, bias_s.shape, 0)
        col = lax.broadcasted_iota(jnp.int32, bias_s.shape, 1)
        bias_s[...] = jnp.where(col % HEADS_PER_GROUP == row // (2 * DEC_SEQ), 0.0, NEG)

    ones = jnp.ones((PAGES_PER_STEP * _SLAB_ROWS, HEAD_DIM), BF16)
    for grp in range(_N_GROUPS):
        lo = grp * HEADS_PER_GROUP
        qz = qz_s[grp].astype(BF16)
        k2 = jnp.concatenate(
            [kp[0, :, lo:lo + HEADS_PER_GROUP, :].reshape(_SLAB_ROWS, HEAD_DIM).astype(BF16)
             for kp in kpages], axis=0)
        v2 = jnp.concatenate(
            [vp[0, :, lo:lo + HEADS_PER_GROUP, :].reshape(_SLAB_ROWS, HEAD_DIM).astype(BF16)
             for vp in vpages], axis=0)
        s = lax.dot_general(qz, k2, (((1,), (1,)), ((), ())), preferred_element_type=F32)
        s = s + bias_s[...]
        _online_update(m_s, acc_s, (grp,), s, jnp.concatenate([v2, ones], axis=1))

    @pl.when(step == last_step)
    def _():
        lam = _lambda(lq1, lk1, lq2, lk2)
        g = g_ref[...]
        out_rows = pl.ds(pl.multiple_of(b * DEC_SEQ, DEC_SEQ), DEC_SEQ)
        for h in range(N_HEADS):
            grp, hl = divmod(h, HEADS_PER_GROUP)
            rows = pl.ds(hl * 2 * DEC_SEQ, 2 * DEC_SEQ)
            cols = slice(h * HEAD_DIM, (h + 1) * HEAD_DIM)
            qh = qz_s[grp, rows, :]
            s = lax.dot_general(qh, kn_ref[:, cols], (((1,), (1,)), ((), ())),
                                preferred_element_type=F32)
            row = lax.broadcasted_iota(jnp.int32, s.shape, 0)
            col = lax.broadcasted_iota(jnp.int32, s.shape, 1)
            causal = col <= row % DEC_SEQ
            s = jnp.where(causal, s, NEG)
            m_prev = m_s[grp, rows, :]
            m_new = jnp.maximum(m_prev, jnp.max(s, axis=-1, keepdims=True))
            p = jnp.where(causal, jnp.exp2(s - m_new[:, :DEC_SEQ]), 0.0)
            alpha = jnp.exp2(m_prev - m_new)
            acc = acc_s[grp, rows, :]
            num = acc[:, :HEAD_DIM] * alpha + jnp.dot(p, vn_ref[:, cols],
                                                      preferred_element_type=F32)
            den = acc[:, HEAD_DIM:] * alpha + jnp.sum(p, axis=-1, keepdims=True)
            out_s[out_rows, cols] = _head_out(num / den, lam, g)

    @pl.when((b == pl.num_programs(0) - 1) & (step == last_step))
    def _():
        o_ref[...] = out_s[...].astype(o_ref.dtype)


def _sample_attn(q, k, v, cache_k, cache_v, page_table, lams, subln_g, attn_all):
    pps = PAGES_PER_STEP

    def new_spec():
        return pl.BlockSpec((DEC_SEQ, ATTN_WIDTH), lambda b, s, pt: (b, 0))

    def page_spec(p):
        return pl.BlockSpec((1, PAGE_SIZE, N_HEADS, HEAD_DIM),
                            lambda b, s, pt: (pt[b, s * pps + p], 0, 0, 0))

    small = pl.BlockSpec((1, HALF_DIM), lambda b, s, pt: (0, 0))
    in_specs = ([new_spec(), new_spec(), new_spec()]
                + [page_spec(p) for p in range(pps)] + [page_spec(p) for p in range(pps)]
                + [small, small, small, small,
                   pl.BlockSpec((1, HEAD_DIM), lambda b, s, pt: (0, 0)),
                   pl.BlockSpec(memory_space=pl.ANY)])
    grid_spec = pltpu.PrefetchScalarGridSpec(
        num_scalar_prefetch=1,
        grid=(DEC_BATCH, N_PAGES // pps),
        in_specs=in_specs,
        out_specs=pl.BlockSpec((M_SAMPLE, ATTN_WIDTH),
                               lambda b, s, pt: (M_PROMPT // M_SAMPLE, 0)),
        scratch_shapes=[pltpu.VMEM((_N_GROUPS, _GROUP_ROWS, HEAD_DIM), F32),
                        pltpu.VMEM((_GROUP_ROWS, pps * _SLAB_ROWS), F32),
                        pltpu.VMEM((_N_GROUPS, _GROUP_ROWS, HEAD_DIM), F32),
                        pltpu.VMEM((_N_GROUPS, _GROUP_ROWS, 2 * HEAD_DIM), F32),
                        pltpu.VMEM((M_SAMPLE, ATTN_WIDTH), F32)],
    )
    return pl.pallas_call(
        _sample_attn_kernel,
        grid_spec=grid_spec,
        out_shape=jax.ShapeDtypeStruct((M_ALL, ATTN_WIDTH), BF16),
        input_output_aliases={len(in_specs): 0},
        compiler_params=_params(("arbitrary", "arbitrary")),
    )(page_table, q, k, v, *([cache_k] * pps), *([cache_v] * pps), *lams,
      subln_g.reshape(1, HEAD_DIM), attn_all)


_HALO = 16


def _pool_groups(ext_ref, lead, t, pos, write):
    for g, w in enumerate(POOL_WINDOWS):
        cols = slice(g * POOL_GROUP, (g + 1) * POOL_GROUP)
        tot = None
        for s in range(w):
            part = ext_ref[lead + (pl.ds(_HALO - s, t), cols)]
            tot = part if tot is None else tot + part
        cnt = jnp.minimum(pos + 1, w).astype(F32)
        pooled = tot / cnt - ext_ref[lead + (pl.ds(_HALO, t), cols)]
        write(g, cols, pooled)


def _pool_prompt_kernel(u_ref, w_ref, scale_ref, o_ref, ext_s, wq_s):
    i = pl.program_id(1)
    t = u_ref.shape[0]

    @pl.when((pl.program_id(0) == 0) & (i == 0))
    def _():
        wq_s[...] = w_ref[...].astype(BF16)

    @pl.when(i == 0)
    def _():
        ext_s[0:_HALO, :] = jnp.zeros((_HALO, POOL_WIDTH), F32)

    @pl.when(i != 0)
    def _():
        ext_s[0:_HALO, :] = ext_s[t:t + _HALO, :]

    ext_s[_HALO:_HALO + t, :] = u_ref[...]
    pos = i * t + lax.broadcasted_iota(jnp.int32, (t, 1), 0)

    def write(g, cols, pooled):
        y = jnp.dot(pooled.astype(BF16), wq_s[g], preferred_element_type=F32)
        o_ref[:, cols] = (y * scale_ref[:, cols]).astype(o_ref.dtype)

    _pool_groups(ext_s, (), t, pos, write)


def _pool_prompt(u, w_pool_mix, pool_scale):
    t = 256
    nt = SEQ // t
    return pl.pallas_call(
        _pool_prompt_kernel,
        grid=(BATCH, nt),
        in_specs=[pl.BlockSpec((t, POOL_WIDTH), lambda b, i: (b * nt + i, 0)),
                  pl.BlockSpec((len(POOL_WINDOWS), POOL_GROUP, POOL_GROUP), lambda b, i: (0, 0, 0)),
                  pl.BlockSpec((1, POOL_WIDTH), lambda b, i: (0, 0))],
        out_specs=pl.BlockSpec((t, POOL_WIDTH), lambda b, i: (b * nt + i, 0)),
        out_shape=jax.ShapeDtypeStruct((M_ALL, POOL_WIDTH), BF16),
        scratch_shapes=[pltpu.VMEM((_HALO + t, POOL_WIDTH), F32),
                        pltpu.VMEM((len(POOL_WINDOWS), POOL_GROUP, POOL_GROUP), BF16)],
        compiler_params=_params(("arbitrary", "arbitrary")),
    )(u, w_pool_mix, pool_scale.reshape(1, POOL_WIDTH))


def _pool_sample_kernel(state_ref, u_ref, w_ref, scale_ref, pooled_in, o_ref, ext_s):
    del pooled_in
    ext_s[:, 0:_HALO, :] = state_ref[...]
    ext_s[:, _HALO:_HALO + DEC_SEQ, :] = u_ref[...]
    pos = PAST_LEN + lax.broadcasted_iota(jnp.int32, (1, DEC_SEQ, 1), 1)

    def write(g, cols, pooled):
        x = pooled.reshape(M_SAMPLE, POOL_GROUP).astype(BF16)
        y = jnp.dot(x, w_ref[g].astype(BF16), preferred_element_type=F32)
        o_ref[:, cols] = (y * scale_ref[:, cols]).astype(o_ref.dtype)

    _pool_groups(ext_s, (slice(None),), DEC_SEQ, pos, write)


def _pool_sample(state16, u3, w_pool_mix, pool_scale, pooled_all):
    ng = len(POOL_WINDOWS)
    return pl.pallas_call(
        _pool_sample_kernel,
        grid=(1,),
        in_specs=[pl.BlockSpec((DEC_BATCH, _HALO, POOL_WIDTH), lambda i: (0, 0, 0)),
                  pl.BlockSpec((DEC_BATCH, DEC_SEQ, POOL_WIDTH), lambda i: (0, 0, 0)),
                  pl.BlockSpec((ng, POOL_GROUP, POOL_GROUP), lambda i: (0, 0, 0)),
                  pl.BlockSpec((1, POOL_WIDTH), lambda i: (0, 0)),
                  pl.BlockSpec(memory_space=pl.ANY)],
        out_specs=pl.BlockSpec((M_SAMPLE, POOL_WIDTH), lambda i: (M_PROMPT // M_SAMPLE, 0)),
        out_shape=jax.ShapeDtypeStruct((M_ALL, POOL_WIDTH), BF16),
        input_output_aliases={4: 0},
        scratch_shapes=[pltpu.VMEM((DEC_BATCH, _HALO + DEC_SEQ, POOL_WIDTH), F32)],
        compiler_params=_params(("arbitrary",)),
    )(state16, u3, w_pool_mix, pool_scale.reshape(1, POOL_WIDTH), pooled_all)


def _rope_tables():
    inv = 1.0 / (ROPE_THETA ** (jnp.arange(0, HALF_DIM, 2, dtype=F32) / HALF_DIM))
    pos = jnp.concatenate([jnp.tile(jnp.arange(SEQ), BATCH),
                           jnp.tile(PAST_LEN + jnp.arange(DEC_SEQ), DEC_BATCH)])
    ang = pos.astype(F32)[:, None] * inv[None, :]
    cos, sin = jnp.cos(ang), jnp.sin(ang)
    cos = jnp.tile(cos, (1, HEAD_DIM // (HALF_DIM // 2)))
    sin = jnp.tile(jnp.concatenate([-sin, sin], axis=1), (1, 2))
    return cos, sin


def _mod_rows(ada):
    ada = ada[:BATCH + DEC_BATCH].reshape(BATCH + DEC_BATCH, N_ADA, D_MODEL)
    prompt = jnp.broadcast_to(ada[:BATCH].transpose(1, 0, 2)[:, :, None, :],
                              (N_ADA, BATCH, ROW_TILE, D_MODEL))
    sample = jnp.repeat(ada[BATCH:], DEC_SEQ, axis=0)
    sample = sample.reshape(M_SAMPLE // ROW_TILE, ROW_TILE, N_ADA, D_MODEL).transpose(2, 0, 1, 3)
    return jnp.concatenate([prompt, sample], axis=1)


def kernel(x_prompt, x_sample, cache_k, cache_v, state_pool, page_table, c_prompt, c_sample,
           w_ada, b_ada, g_pre_ffn1, w1_ffn1, w3_ffn1, w2_ffn1, g_post_ffn1,
           g_pre_mix, w_in, lambda_q1, lambda_k1, lambda_q2, lambda_k2, subln_g,
           w_pool_mix, pool_scale, w_up_attn, w_up_pool, w_merge_gate, w_out, g_post_mix,
           g_pre_ffn2, w1_ffn2, w3_ffn2, w2_ffn2, g_post_ffn2):
    x = (x_prompt.reshape(M_PROMPT, D_MODEL), x_sample.reshape(M_SAMPLE, D_MODEL))
    c = jnp.concatenate([c_prompt, c_sample])
    c = jnp.pad(c, ((0, 48 - c.shape[0]), (0, 0)))
    mod = _mod_rows(_ada(c, w_ada, b_ada))
    lams = [v.reshape(1, HALF_DIM) for v in (lambda_q1, lambda_k1, lambda_q2, lambda_k2)]

    h = _norm_mod(x, g_pre_ffn1, mod, 0)
    f = _down(_swiglu_up(h, w1_ffn1, w3_ffn1), w2_ffn1)
    x, h = _resid_norm_mod(x, f, g_post_ffn1, g_pre_mix, mod, 0, 1, 0.5)

    tabs = _rope_tables()
    q = _proj(h, w_in, 0, ATTN_WIDTH, BF16, tabs, HALF_DIM ** -0.5 * LOG2_E)
    k = _proj(h, w_in, ATTN_WIDTH, ATTN_WIDTH, F32, tabs)
    v = _proj(h, w_in, 2 * ATTN_WIDTH, ATTN_WIDTH, F32)
    u = _proj(h, w_in, 3 * ATTN_WIDTH, POOL_WIDTH, F32)

    attn = _prompt_attn(q, k, v, lams, subln_g)
    attn = _sample_attn(q[M_PROMPT:].astype(F32), k[M_PROMPT:], v[M_PROMPT:],
                        cache_k, cache_v, page_table, lams, subln_g, attn)

    u_sample = u[M_PROMPT:].reshape(DEC_BATCH, DEC_SEQ, POOL_WIDTH)
    state16 = jnp.pad(state_pool, ((0, 0), (_HALO - POOL_BUF, 0), (0, 0)))
    pooled = _pool_prompt(u, w_pool_mix, pool_scale)
    pooled = _pool_sample(state16, u_sample, w_pool_mix, pool_scale, pooled)

    merged = _merge(h, attn, pooled, w_merge_gate, w_up_attn, w_up_pool)
    m = _proj(merged, w_out, 0, D_MODEL, F32)
    x, h = _resid_norm_mod(x, m, g_post_mix, g_pre_ffn2, mod, 1, 2, 1.0)

    f = _down(_swiglu_up(h, w1_ffn2, w3_ffn2), w2_ffn2)
    y_prompt, y_sample = _resid(x, f, g_post_ffn2, mod, 2, 0.5)
    y_prompt = y_prompt.reshape(BATCH, SEQ, D_MODEL)
    y_sample = y_sample.reshape(DEC_BATCH, DEC_SEQ, D_MODEL)
    k_prompt = k[:M_PROMPT].reshape(BATCH, SEQ, N_HEADS, HEAD_DIM)
    v_prompt = v[:M_PROMPT].reshape(BATCH, SEQ, N_HEADS, HEAD_DIM)
    k_sample = k[M_PROMPT:].reshape(DEC_BATCH, DEC_SEQ, N_HEADS, HEAD_DIM)
    v_sample = v[M_PROMPT:].reshape(DEC_BATCH, DEC_SEQ, N_HEADS, HEAD_DIM)
    pool_prompt = u[:M_PROMPT].reshape(BATCH, SEQ, POOL_WIDTH)[:, SEQ - POOL_BUF:]
    pool_sample = jnp.concatenate([state_pool, u_sample], axis=1)[:, -POOL_BUF:]
    return (y_prompt, y_sample, k_prompt, v_prompt, pool_prompt, k_sample, v_sample, pool_sample)
```

```python
import functools
import math

import jax
import jax.numpy as jnp
from jax import lax
from jax.experimental import pallas as pl
from jax.experimental.pallas import tpu as pltpu

F32 = jnp.float32
BF16 = jnp.bfloat16

D_MODEL = 4096
BATCH = 2
SEQ = 4096
DEC_BATCH = 32
DEC_SEQ = 8
PAST_LEN = 8192
PAGE_SIZE = 128
N_PAGES = PAST_LEN // PAGE_SIZE
ATTN_WIDTH = D_MODEL // 2
HALF_DIM = 64
HEAD_DIM = 2 * HALF_DIM
N_HEADS = ATTN_WIDTH // HEAD_DIM
POOL_WIDTH = D_MODEL - ATTN_WIDTH
POOL_WINDOWS = (2, 4, 8, 16)
POOL_GROUP = POOL_WIDTH // len(POOL_WINDOWS)
POOL_BUF = max(POOL_WINDOWS) - 1
D_FF = ((8 * D_MODEL // 3 + 255) // 256) * 256
ROPE_THETA = 10000.0
N_ADA = 9
EPS = 1e-6
NEG = -1e30
LAM_INIT = 0.8 - 0.6 * math.exp(-0.3 * 0)

M_PROMPT = BATCH * SEQ
M_SAMPLE = DEC_BATCH * DEC_SEQ
M_ALL = M_PROMPT + M_SAMPLE

VMEM_LIMIT_BYTES = 58 * 1024 * 1024

ROW_TILE = 128
MM_ROWS = 768
UP_ROWS = 1408
EPILOGUE_CHUNKS = 2
LOG2_E = math.log2(math.e)
MERGE_ROWS = 384
PAGES_PER_STEP = 8
HEADS_PER_GROUP = 8
Q_TILE = 512
Q_ROW_BLOCKS = 2


def _params(sem):
    return pltpu.CompilerParams(dimension_semantics=sem, vmem_limit_bytes=VMEM_LIMIT_BYTES)


def _sigmoid(x):
    return 1.0 / (1.0 + jnp.exp(-x))


def _rms(x, g):
    return x * lax.rsqrt(jnp.mean(x * x, axis=-1, keepdims=True) + EPS) * g


def _ada_kernel(c_ref, w_ref, b_ref, o_ref):
    c = c_ref[...]
    a = (c * _sigmoid(c)).astype(BF16)
    o_ref[...] = jnp.dot(a, w_ref[...].astype(BF16), preferred_element_type=F32) + b_ref[...]


def _ada(c_pad, w_ada, b_ada):
    rows = c_pad.shape[0]
    n = w_ada.shape[1]
    tn = 1024
    return pl.pallas_call(
        _ada_kernel,
        grid=(n // tn,),
        in_specs=[pl.BlockSpec((rows, D_MODEL), lambda j: (0, 0)),
                  pl.BlockSpec((D_MODEL, tn), lambda j: (0, j)),
                  pl.BlockSpec((1, tn), lambda j: (0, j))],
        out_specs=pl.BlockSpec((rows, tn), lambda j: (0, j)),
        out_shape=jax.ShapeDtypeStruct((rows, n), F32),
        compiler_params=_params(("arbitrary",)),
    )(c_pad, w_ada, b_ada.reshape(1, n))


_PROMPT_TILES = M_PROMPT // ROW_TILE
_TILES_PER_BATCH = SEQ // ROW_TILE


def _mod_group(i):
    return jnp.where(i < _PROMPT_TILES, i // _TILES_PER_BATCH, i - _PROMPT_TILES + BATCH)


def _mod_spec(j):
    return pl.BlockSpec((1, 1, ROW_TILE, D_MODEL), lambda i: (j, _mod_group(i), 0, 0))


def _row_spec():
    return pl.BlockSpec((ROW_TILE, D_MODEL), lambda i: (i, 0))


def _pair_specs():
    return [pl.BlockSpec((ROW_TILE, D_MODEL), lambda i: (jnp.minimum(i, _PROMPT_TILES - 1), 0)),
            pl.BlockSpec((ROW_TILE, D_MODEL), lambda i: (jnp.maximum(i - _PROMPT_TILES, 0), 0))]


def _pair_shapes():
    return [jax.ShapeDtypeStruct((M_PROMPT, D_MODEL), F32),
            jax.ShapeDtypeStruct((M_SAMPLE, D_MODEL), F32)]


def _vec_spec():
    return pl.BlockSpec((1, D_MODEL), lambda i: (0, 0))


def _read_pair(xp_ref, xs_ref):
    return jnp.where(pl.program_id(0) < _PROMPT_TILES, xp_ref[...], xs_ref[...])


def _write_pair(xp_ref, xs_ref, x):
    @pl.when(pl.program_id(0) < _PROMPT_TILES)
    def _():
        xp_ref[...] = x

    @pl.when(pl.program_id(0) >= _PROMPT_TILES)
    def _():
        xs_ref[...] = x


def _norm_mod_kernel(xp_ref, xs_ref, g_ref, scale_ref, shift_ref, h_ref):
    h = _rms(_read_pair(xp_ref, xs_ref), g_ref[...])
    h_ref[...] = (h * (1.0 + scale_ref[0, 0]) + shift_ref[0, 0]).astype(h_ref.dtype)


def _norm_mod(x, g, mod, j):
    return pl.pallas_call(
        _norm_mod_kernel,
        grid=(M_ALL // ROW_TILE,),
        in_specs=_pair_specs() + [_vec_spec(), _mod_spec(3 * j + 1), _mod_spec(3 * j)],
        out_specs=_row_spec(),
        out_shape=jax.ShapeDtypeStruct((M_ALL, D_MODEL), BF16),
        compiler_params=_params(("arbitrary",)),
    )(*x, g.reshape(1, D_MODEL), mod, mod)


def _resid_norm_mod_kernel(xp_ref, xs_ref, f_ref, gpost_ref, gate_ref, gpre_ref, scale_ref,
                           shift_ref, xpo_ref, xso_ref, h_ref, *, coef):
    x = _read_pair(xp_ref, xs_ref) + coef * gate_ref[0, 0] * _rms(f_ref[...], gpost_ref[...])
    _write_pair(xpo_ref, xso_ref, x)
    h = _rms(x, gpre_ref[...])
    h_ref[...] = (h * (1.0 + scale_ref[0, 0]) + shift_ref[0, 0]).astype(h_ref.dtype)


def _resid_norm_mod(x, f, gpost, gpre, mod, j_prev, j_next, coef):
    xp, xs, h = pl.pallas_call(
        functools.partial(_resid_norm_mod_kernel, coef=coef),
        grid=(M_ALL // ROW_TILE,),
        in_specs=_pair_specs() + [_row_spec(), _vec_spec(), _mod_spec(3 * j_prev + 2), _vec_spec(),
                                  _mod_spec(3 * j_next + 1), _mod_spec(3 * j_next)],
        out_specs=_pair_specs() + [_row_spec()],
        out_shape=_pair_shapes() + [jax.ShapeDtypeStruct((M_ALL, D_MODEL), BF16)],
        compiler_params=_params(("arbitrary",)),
    )(*x, f, gpost.reshape(1, D_MODEL), mod, gpre.reshape(1, D_MODEL), mod, mod)
    return (xp, xs), h


def _resid_kernel(xp_ref, xs_ref, f_ref, gpost_ref, gate_ref, xpo_ref, xso_ref, *, coef):
    x = _read_pair(xp_ref, xs_ref) + coef * gate_ref[0, 0] * _rms(f_ref[...], gpost_ref[...])
    _write_pair(xpo_ref, xso_ref, x)


def _resid(x, f, gpost, mod, j_prev, coef):
    return pl.pallas_call(
        functools.partial(_resid_kernel, coef=coef),
        grid=(M_ALL // ROW_TILE,),
        in_specs=_pair_specs() + [_row_spec(), _vec_spec(), _mod_spec(3 * j_prev + 2)],
        out_specs=_pair_specs(),
        out_shape=_pair_shapes(),
        compiler_params=_params(("arbitrary",)),
    )(*x, f, gpost.reshape(1, D_MODEL), mod)


def _swiglu_up_kernel(a_ref, w1_ref, w3_ref, o_ref, w1q, w3q):
    @pl.when(pl.program_id(1) == 0)
    def _():
        w1q[...] = w1_ref[...].astype(BF16)
        w3q[...] = w3_ref[...].astype(BF16)

    rows = a_ref.shape[0] // EPILOGUE_CHUNKS
    for c in range(EPILOGUE_CHUNKS):
        sl = pl.ds(c * rows, rows)
        a = a_ref[sl, :]
        u = jnp.dot(a, w1q[...], preferred_element_type=F32)
        v = jnp.dot(a, w3q[...], preferred_element_type=F32)
        o_ref[sl, :] = (u * _sigmoid(u) * v).astype(o_ref.dtype)


def _swiglu_up(h, w1, w3):
    tn = 256
    tm = UP_ROWS
    return pl.pallas_call(
        _swiglu_up_kernel,
        grid=(D_FF // tn, M_ALL // tm),
        in_specs=[pl.BlockSpec((tm, D_MODEL), lambda j, i: (i, 0)),
                  pl.BlockSpec((D_MODEL, tn), lambda j, i: (0, j)),
                  pl.BlockSpec((D_MODEL, tn), lambda j, i: (0, j))],
        out_specs=pl.BlockSpec((tm, tn), lambda j, i: (i, j)),
        out_shape=jax.ShapeDtypeStruct((M_ALL, D_FF), BF16),
        scratch_shapes=[pltpu.VMEM((D_MODEL, tn), BF16), pltpu.VMEM((D_MODEL, tn), BF16)],
        compiler_params=_params(("arbitrary", "arbitrary")),
    )(h, w1, w3)


def _down_kernel(a0_ref, a1_ref, w0_ref, w1_ref, o_ref):
    k = pl.program_id(1)
    last = pl.num_programs(1) - 1

    def first():
        return jnp.dot(a0_ref[...], w0_ref[...].astype(BF16), preferred_element_type=F32)

    def second():
        return jnp.dot(a1_ref[...], w1_ref[...].astype(BF16), preferred_element_type=F32)

    @pl.when(k == 0)
    def _():
        o_ref[...] = first() + second()

    @pl.when((k != 0) & (k != last))
    def _():
        o_ref[...] += first() + second()

    @pl.when(k == last)
    def _():
        o_ref[...] += first()


def _down(g, w2):
    tk = 256
    tm = MM_ROWS
    nk = D_FF // tk
    assert nk % 2 == 1 and nk > 2
    return pl.pallas_call(
        _down_kernel,
        grid=(M_ALL // tm, (nk + 1) // 2),
        in_specs=[pl.BlockSpec((tm, tk), lambda i, k: (i, 2 * k)),
                  pl.BlockSpec((tm, tk), lambda i, k: (i, jnp.minimum(2 * k + 1, nk - 1))),
                  pl.BlockSpec((tk, D_MODEL), lambda i, k: (2 * k, 0)),
                  pl.BlockSpec((tk, D_MODEL), lambda i, k: (jnp.minimum(2 * k + 1, nk - 1), 0))],
        out_specs=pl.BlockSpec((tm, D_MODEL), lambda i, k: (i, 0)),
        out_shape=jax.ShapeDtypeStruct((M_ALL, D_MODEL), F32),
        compiler_params=_params(("arbitrary", "arbitrary")),
    )(g, g, w2, w2)


def _proj_kernel(a_ref, w_ref, *rest, rope, out_scale):
    if rope:
        cos_ref, sin_ref, o_ref, wq = rest
    else:
        o_ref, wq = rest

    @pl.when(pl.program_id(1) == 0)
    def _():
        wq[...] = w_ref[...].astype(BF16)

    rows = a_ref.shape[0] // EPILOGUE_CHUNKS
    tn = o_ref.shape[1]
    for c in range(EPILOGUE_CHUNKS):
        sl = pl.ds(c * rows, rows)
        y = jnp.dot(a_ref[sl, :], wq[...], preferred_element_type=F32)
        if rope:
            reps = tn // HEAD_DIM
            cos = jnp.tile(cos_ref[sl, :], (1, reps))
            sin = jnp.tile(sin_ref[sl, :], (1, reps))
            lane = lax.broadcasted_iota(jnp.int32, y.shape, 1)
            first = (lane % HALF_DIM) < (HALF_DIM // 2)
            partner = jnp.where(first, pltpu.roll(y, tn - HALF_DIM // 2, 1),
                                pltpu.roll(y, HALF_DIM // 2, 1))
            y = y * cos + partner * sin
        if out_scale != 1.0:
            y = y * out_scale
        o_ref[sl, :] = y.astype(o_ref.dtype)


def _proj(a, w, col0, n, out_dtype, rows, rope_tabs=None, out_scale=1.0):
    row0, m, tm = rows
    k = a.shape[1]
    tn = 512
    joff = col0 // tn
    ioff = row0 // tm
    assert row0 % tm == 0 and m % tm == 0 and col0 % tn == 0 and n % tn == 0
    in_specs = [pl.BlockSpec((tm, k), lambda j, i: (i + ioff, 0)),
                pl.BlockSpec((k, tn), lambda j, i: (0, j + joff))]
    args = [a, w]
    if rope_tabs is not None:
        in_specs += [pl.BlockSpec((tm, HEAD_DIM), lambda j, i: (i + ioff, 0))] * 2
        args += list(rope_tabs)
    return pl.pallas_call(
        functools.partial(_proj_kernel, rope=rope_tabs is not None, out_scale=out_scale),
        grid=(n // tn, m // tm),
        in_specs=in_specs,
        out_specs=pl.BlockSpec((tm, tn), lambda j, i: (i, j)),
        out_shape=jax.ShapeDtypeStruct((m, n), out_dtype),
        scratch_shapes=[pltpu.VMEM((k, tn), BF16)],
        compiler_params=_params(("arbitrary", "arbitrary")),
    )(*args)


_ALL_ROWS = (0, M_ALL, MM_ROWS)
_PROMPT_ROWS = (0, M_PROMPT, 1024)
_SAMPLE_ROWS = (M_PROMPT, M_SAMPLE, M_SAMPLE)


def _merge_kernel(h_ref, at_ref, po_ref, wg0_ref, wg1_ref, wa_ref, wp_ref, o_ref,
                  wg0q, wg1q, waq, wpq):
    @pl.when(pl.program_id(1) == 0)
    def _():
        wg0q[...] = wg0_ref[...].astype(BF16)
        wg1q[...] = wg1_ref[...].astype(BF16)
        waq[...] = wa_ref[...].astype(BF16)
        wpq[...] = wp_ref[...].astype(BF16)

    rows = h_ref.shape[0] // EPILOGUE_CHUNKS
    for c in range(EPILOGUE_CHUNKS):
        sl = pl.ds(c * rows, rows)
        h = h_ref[sl, :]
        g0 = _sigmoid(jnp.dot(h, wg0q[...], preferred_element_type=F32))
        g1 = _sigmoid(jnp.dot(h, wg1q[...], preferred_element_type=F32))
        a_up = jnp.dot(at_ref[sl, :], waq[...], preferred_element_type=F32)
        p_up = jnp.dot(po_ref[sl, :], wpq[...], preferred_element_type=F32)
        o_ref[sl, :] = (g0 * a_up + g1 * p_up).astype(o_ref.dtype)


def _merge(h, attn, pooled, w_gate, w_up_attn, w_up_pool):
    tn = 256
    tm = MERGE_ROWS
    goff = D_MODEL // tn
    return pl.pallas_call(
        _merge_kernel,
        grid=(D_MODEL // tn, M_ALL // tm),
        in_specs=[pl.BlockSpec((tm, D_MODEL), lambda j, i: (i, 0)),
                  pl.BlockSpec((tm, ATTN_WIDTH), lambda j, i: (i, 0)),
                  pl.BlockSpec((tm, POOL_WIDTH), lambda j, i: (i, 0)),
                  pl.BlockSpec((D_MODEL, tn), lambda j, i: (0, j)),
                  pl.BlockSpec((D_MODEL, tn), lambda j, i: (0, j + goff)),
                  pl.BlockSpec((ATTN_WIDTH, tn), lambda j, i: (0, j)),
                  pl.BlockSpec((POOL_WIDTH, tn), lambda j, i: (0, j))],
        out_specs=pl.BlockSpec((tm, tn), lambda j, i: (i, j)),
        out_shape=jax.ShapeDtypeStruct((M_ALL, D_MODEL), BF16),
        scratch_shapes=[pltpu.VMEM((D_MODEL, tn), BF16), pltpu.VMEM((D_MODEL, tn), BF16),
                        pltpu.VMEM((ATTN_WIDTH, tn), BF16), pltpu.VMEM((POOL_WIDTH, tn), BF16)],
        compiler_params=_params(("arbitrary", "arbitrary")),
    )(h, attn, pooled, w_gate, w_gate, w_up_attn, w_up_pool)


def _lambda(lq1, lk1, lq2, lk2):
    return (jnp.exp(jnp.sum(lq1[...] * lk1[...], axis=-1, keepdims=True))
            - jnp.exp(jnp.sum(lq2[...] * lk2[...], axis=-1, keepdims=True)) + LAM_INIT)


def _split_halves(q):
    lane = lax.broadcasted_iota(jnp.int32, q.shape, 1)
    zero = jnp.zeros_like(q)
    return jnp.concatenate([jnp.where(lane < HALF_DIM, q, zero),
                            jnp.where(lane >= HALF_DIM, q, zero)], axis=0)


def _head_out(o, lam, g):
    t = o.shape[0] // 2
    a = o[:t] - lam * o[t:]
    return _rms(a, g) * (1.0 - LAM_INIT)


def _online_update(m_ref, acc_ref, idx, s, v_ones):
    m_prev = m_ref[idx]
    m_new = jnp.maximum(m_prev, jnp.max(s, axis=-1, keepdims=True))
    p = jnp.exp2(s - jnp.tile(m_new, (1, s.shape[1] // HEAD_DIM)))
    alpha = jnp.exp2(m_prev - m_new)
    acc_ref[idx] = (acc_ref[idx] * jnp.tile(alpha, (1, 2))
                    + jnp.dot(p.astype(BF16), v_ones, preferred_element_type=F32))
    m_ref[idx] = m_new


def _prompt_attn_kernel(q_ref, k_ref, v_ref, lq1, lk1, lq2, lk2, g_ref, o_ref,
                        kb, vb, m_s, acc_s):
    qi = pl.program_id(2)
    tq = q_ref.shape[0]

    @pl.when(qi == 0)
    def _():
        kb[...] = k_ref[...].astype(BF16)
        vb[:, :HEAD_DIM] = v_ref[...].astype(BF16)
        vb[:, HEAD_DIM:] = jnp.ones((SEQ, HEAD_DIM), BF16)

    n_chains, rows = m_s.shape[0], m_s.shape[1]
    row_blocks = n_chains // 2
    chains = []
    for c in range(n_chains):
        r0 = (c % row_blocks) * rows
        q = q_ref[pl.ds(r0, rows), :]
        lane = lax.broadcasted_iota(jnp.int32, q.shape, 1)
        keep = (lane < HALF_DIM) if c < row_blocks else (lane >= HALF_DIM)
        chains.append((r0, jnp.where(keep, q, jnp.zeros_like(q))))
    m_s[...] = jnp.full(m_s.shape, NEG, F32)
    acc_s[...] = jnp.zeros(acc_s.shape, F32)

    def step(j, masked):
        start = pl.multiple_of(j * tq, tq)
        k = kb[pl.ds(start, tq), :]
        v = vb[pl.ds(start, tq), :]
        for c, (r0, qc) in enumerate(chains):
            nk = r0 + rows if masked else tq
            s = lax.dot_general(qc, k[:nk], (((1,), (1,)), ((), ())), preferred_element_type=F32)
            if masked:
                row = r0 + lax.broadcasted_iota(jnp.int32, s.shape, 0)
                col = lax.broadcasted_iota(jnp.int32, s.shape, 1)
                s = jnp.where(col <= row, s, NEG)
            _online_update(m_s, acc_s, (c,), s, v[:nk])

    def body(j, carry):
        step(j, False)
        return carry

    lax.fori_loop(0, qi, body, 0)
    step(qi, True)

    lam = _lambda(lq1, lk1, lq2, lk2)
    for rb in range(row_blocks):
        acc = jnp.concatenate([acc_s[rb], acc_s[row_blocks + rb]], axis=0)
        o = acc[:, :HEAD_DIM] / acc[:, HEAD_DIM:]
        o_ref[pl.ds(rb * rows, rows), :] = _head_out(o, lam, g_ref[...]).astype(o_ref.dtype)


def _prompt_attn(q, k, v, lams, subln_g):
    tq = Q_TILE
    nq = SEQ // tq
    small = pl.BlockSpec((1, HALF_DIM), lambda b, h, i: (0, 0))
    return pl.pallas_call(
        _prompt_attn_kernel,
        grid=(BATCH, N_HEADS, nq),
        in_specs=[pl.BlockSpec((tq, HEAD_DIM), lambda b, h, i: (b * nq + i, h)),
                  pl.BlockSpec((SEQ, HEAD_DIM), lambda b, h, i: (b, h)),
                  pl.BlockSpec((SEQ, HEAD_DIM), lambda b, h, i: (b, h)),
                  small, small, small, small,
                  pl.BlockSpec((1, HEAD_DIM), lambda b, h, i: (0, 0))],
        out_specs=pl.BlockSpec((tq, HEAD_DIM), lambda b, h, i: (b * nq + i, h)),
        out_shape=jax.ShapeDtypeStruct((M_ALL, ATTN_WIDTH), BF16),
        scratch_shapes=[pltpu.VMEM((SEQ, HEAD_DIM), BF16), pltpu.VMEM((SEQ, 2 * HEAD_DIM), BF16),
                        pltpu.VMEM((2 * Q_ROW_BLOCKS, tq // Q_ROW_BLOCKS, HEAD_DIM), F32),
                        pltpu.VMEM((2 * Q_ROW_BLOCKS, tq // Q_ROW_BLOCKS, 2 * HEAD_DIM), F32)],
        compiler_params=_params(("arbitrary", "arbitrary", "arbitrary")),
    )(q, k, v, *lams, subln_g.reshape(1, HEAD_DIM))


_GROUP_ROWS = HEADS_PER_GROUP * 2 * DEC_SEQ
_SLAB_ROWS = PAGE_SIZE * HEADS_PER_GROUP
_N_GROUPS = N_HEADS // HEADS_PER_GROUP


def _sample_attn_kernel(pt_ref, q_ref, kn_ref, vn_ref, *rest):
    del pt_ref
    kpages = rest[:PAGES_PER_STEP]
    vpages = rest[PAGES_PER_STEP:2 * PAGES_PER_STEP]
    (lq1, lk1, lq2, lk2, g_ref, attn_in, o_ref,
     qz_s, bias_s, m_s, acc_s, out_s) = rest[2 * PAGES_PER_STEP:]
    del attn_in
    b = pl.program_id(0)
    step = pl.program_id(1)
    last_step = pl.num_programs(1) - 1

    @pl.when(step == 0)
    def _():
        for h in range(N_HEADS):
            grp, hl = divmod(h, HEADS_PER_GROUP)
            qh = q_ref[:, h * HEAD_DIM:(h + 1) * HEAD_DIM]
            qz_s[grp, hl * 2 * DEC_SEQ:(hl + 1) * 2 * DEC_SEQ, :] = _split_halves(qh)
        m_s[...] = jnp.full(m_s.shape, NEG, F32)
        acc_s[...] = jnp.zeros(acc_s.shape, F32)

    @pl.when((b == 0) & (step == 0))
    def _():
        row = lax.broadcasted_iota(jnp.int32, bias_s.shape, 0)
        col = lax.broadcasted_iota(jnp.int32, bias_s.shape, 1)
        bias_s[...] = jnp.where(col % HEADS_PER_GROUP == row // (2 * DEC_SEQ), 0.0, NEG)

    ones = jnp.ones((PAGES_PER_STEP * _SLAB_ROWS, HEAD_DIM), BF16)
    for grp in range(_N_GROUPS):
        lo = grp * HEADS_PER_GROUP
        qz = qz_s[grp].astype(BF16)
        k2 = jnp.concatenate(
            [kp[0, :, lo:lo + HEADS_PER_GROUP, :].reshape(_SLAB_ROWS, HEAD_DIM).astype(BF16)
             for kp in kpages], axis=0)
        v2 = jnp.concatenate(
            [vp[0, :, lo:lo + HEADS_PER_GROUP, :].reshape(_SLAB_ROWS, HEAD_DIM).astype(BF16)
             for vp in vpages], axis=0)
        s = lax.dot_general(qz, k2, (((1,), (1,)), ((), ())), preferred_element_type=F32)
        s = s + bias_s[...]
        _online_update(m_s, acc_s, (grp,), s, jnp.concatenate([v2, ones], axis=1))

    @pl.when(step == last_step)
    def _():
        lam = _lambda(lq1, lk1, lq2, lk2)
        g = g_ref[...]
        out_rows = pl.ds(pl.multiple_of(b * DEC_SEQ, DEC_SEQ), DEC_SEQ)
        for h in range(N_HEADS):
            grp, hl = divmod(h, HEADS_PER_GROUP)
            rows = pl.ds(hl * 2 * DEC_SEQ, 2 * DEC_SEQ)
            cols = slice(h * HEAD_DIM, (h + 1) * HEAD_DIM)
            qh = qz_s[grp, rows, :]
            s = lax.dot_general(qh, kn_ref[:, cols], (((1,), (1,)), ((), ())),
                                preferred_element_type=F32)
            row = lax.broadcasted_iota(jnp.int32, s.shape, 0)
            col = lax.broadcasted_iota(jnp.int32, s.shape, 1)
            causal = col <= row % DEC_SEQ
            s = jnp.where(causal, s, NEG)
            m_prev = m_s[grp, rows, :]
            m_new = jnp.maximum(m_prev, jnp.max(s, axis=-1, keepdims=True))
            p = jnp.where(causal, jnp.exp2(s - m_new[:, :DEC_SEQ]), 0.0)
            alpha = jnp.exp2(m_prev - m_new)
            acc = acc_s[grp, rows, :]
            num = acc[:, :HEAD_DIM] * alpha + jnp.dot(p, vn_ref[:, cols],
                                                      preferred_element_type=F32)
            den = acc[:, HEAD_DIM:] * alpha + jnp.sum(p, axis=-1, keepdims=True)
            out_s[out_rows, cols] = _head_out(num / den, lam, g)

    @pl.when((b == pl.num_programs(0) - 1) & (step == last_step))
    def _():
        o_ref[...] = out_s[...].astype(o_ref.dtype)


def _sample_attn(q, k, v, cache_k, cache_v, page_table, lams, subln_g, attn_all):
    pps = PAGES_PER_STEP

    def new_spec():
        return pl.BlockSpec((DEC_SEQ, ATTN_WIDTH), lambda b, s, pt: (b, 0))

    def page_spec(p):
        return pl.BlockSpec((1, PAGE_SIZE, N_HEADS, HEAD_DIM),
                            lambda b, s, pt: (pt[b, s * pps + p], 0, 0, 0))

    small = pl.BlockSpec((1, HALF_DIM), lambda b, s, pt: (0, 0))
    in_specs = ([new_spec(), new_spec(), new_spec()]
                + [page_spec(p) for p in range(pps)] + [page_spec(p) for p in range(pps)]
                + [small, small, small, small,
                   pl.BlockSpec((1, HEAD_DIM), lambda b, s, pt: (0, 0)),
                   pl.BlockSpec(memory_space=pl.ANY)])
    grid_spec = pltpu.PrefetchScalarGridSpec(
        num_scalar_prefetch=1,
        grid=(DEC_BATCH, N_PAGES // pps),
        in_specs=in_specs,
        out_specs=pl.BlockSpec((M_SAMPLE, ATTN_WIDTH),
                               lambda b, s, pt: (M_PROMPT // M_SAMPLE, 0)),
        scratch_shapes=[pltpu.VMEM((_N_GROUPS, _GROUP_ROWS, HEAD_DIM), F32),
                        pltpu.VMEM((_GROUP_ROWS, pps * _SLAB_ROWS), F32),
                        pltpu.VMEM((_N_GROUPS, _GROUP_ROWS, HEAD_DIM), F32),
                        pltpu.VMEM((_N_GROUPS, _GROUP_ROWS, 2 * HEAD_DIM), F32),
                        pltpu.VMEM((M_SAMPLE, ATTN_WIDTH), F32)],
    )
    return pl.pallas_call(
        _sample_attn_kernel,
        grid_spec=grid_spec,
        out_shape=jax.ShapeDtypeStruct((M_ALL, ATTN_WIDTH), BF16),
        input_output_aliases={len(in_specs): 0},
        compiler_params=_params(("arbitrary", "arbitrary")),
    )(page_table, q, k, v, *([cache_k] * pps), *([cache_v] * pps), *lams,
      subln_g.reshape(1, HEAD_DIM), attn_all)


_HALO = 16


def _pool_groups(ext_ref, lead, t, pos, write):
    for g, w in enumerate(POOL_WINDOWS):
        cols = slice(g * POOL_GROUP, (g + 1) * POOL_GROUP)
        tot = None
        for s in range(w):
            part = ext_ref[lead + (pl.ds(_HALO - s, t), cols)]
            tot = part if tot is None else tot + part
        cnt = jnp.minimum(pos + 1, w).astype(F32)
        pooled = tot / cnt - ext_ref[lead + (pl.ds(_HALO, t), cols)]
        write(g, cols, pooled)


def _pool_prompt_kernel(u_ref, w_ref, scale_ref, o_ref, ext_s, wq_s):
    i = pl.program_id(1)
    t = u_ref.shape[0]

    @pl.when((pl.program_id(0) == 0) & (i == 0))
    def _():
        wq_s[...] = w_ref[...].astype(BF16)

    @pl.when(i == 0)
    def _():
        ext_s[0:_HALO, :] = jnp.zeros((_HALO, POOL_WIDTH), F32)

    @pl.when(i != 0)
    def _():
        ext_s[0:_HALO, :] = ext_s[t:t + _HALO, :]

    ext_s[_HALO:_HALO + t, :] = u_ref[...]
    pos = i * t + lax.broadcasted_iota(jnp.int32, (t, 1), 0)

    def write(g, cols, pooled):
        y = jnp.dot(pooled.astype(BF16), wq_s[g], preferred_element_type=F32)
        o_ref[:, cols] = (y * scale_ref[:, cols]).astype(o_ref.dtype)

    _pool_groups(ext_s, (), t, pos, write)


def _pool_prompt(u, w_pool_mix, pool_scale):
    t = 256
    nt = SEQ // t
    return pl.pallas_call(
        _pool_prompt_kernel,
        grid=(BATCH, nt),
        in_specs=[pl.BlockSpec((t, POOL_WIDTH), lambda b, i: (b * nt + i, 0)),
                  pl.BlockSpec((len(POOL_WINDOWS), POOL_GROUP, POOL_GROUP), lambda b, i: (0, 0, 0)),
                  pl.BlockSpec((1, POOL_WIDTH), lambda b, i: (0, 0))],
        out_specs=pl.BlockSpec((t, POOL_WIDTH), lambda b, i: (b * nt + i, 0)),
        out_shape=jax.ShapeDtypeStruct((M_ALL, POOL_WIDTH), BF16),
        scratch_shapes=[pltpu.VMEM((_HALO + t, POOL_WIDTH), F32),
                        pltpu.VMEM((len(POOL_WINDOWS), POOL_GROUP, POOL_GROUP), BF16)],
        compiler_params=_params(("arbitrary", "arbitrary")),
    )(u, w_pool_mix, pool_scale.reshape(1, POOL_WIDTH))


def _pool_sample_kernel(state_ref, u_ref, w_ref, scale_ref, pooled_in, o_ref, ext_s):
    del pooled_in
    ext_s[:, 0:_HALO, :] = state_ref[...]
    ext_s[:, _HALO:_HALO + DEC_SEQ, :] = u_ref[...]
    pos = PAST_LEN + lax.broadcasted_iota(jnp.int32, (1, DEC_SEQ, 1), 1)

    def write(g, cols, pooled):
        x = pooled.reshape(M_SAMPLE, POOL_GROUP).astype(BF16)
        y = jnp.dot(x, w_ref[g].astype(BF16), preferred_element_type=F32)
        o_ref[:, cols] = (y * scale_ref[:, cols]).astype(o_ref.dtype)

    _pool_groups(ext_s, (slice(None),), DEC_SEQ, pos, write)


def _pool_sample(state16, u3, w_pool_mix, pool_scale, pooled_all):
    ng = len(POOL_WINDOWS)
    return pl.pallas_call(
        _pool_sample_kernel,
        grid=(1,),
        in_specs=[pl.BlockSpec((DEC_BATCH, _HALO, POOL_WIDTH), lambda i: (0, 0, 0)),
                  pl.BlockSpec((DEC_BATCH, DEC_SEQ, POOL_WIDTH), lambda i: (0, 0, 0)),
                  pl.BlockSpec((ng, POOL_GROUP, POOL_GROUP), lambda i: (0, 0, 0)),
                  pl.BlockSpec((1, POOL_WIDTH), lambda i: (0, 0)),
                  pl.BlockSpec(memory_space=pl.ANY)],
        out_specs=pl.BlockSpec((M_SAMPLE, POOL_WIDTH), lambda i: (M_PROMPT // M_SAMPLE, 0)),
        out_shape=jax.ShapeDtypeStruct((M_ALL, POOL_WIDTH), BF16),
        input_output_aliases={4: 0},
        scratch_shapes=[pltpu.VMEM((DEC_BATCH, _HALO + DEC_SEQ, POOL_WIDTH), F32)],
        compiler_params=_params(("arbitrary",)),
    )(state16, u3, w_pool_mix, pool_scale.reshape(1, POOL_WIDTH), pooled_all)


def _rope_tables():
    inv = 1.0 / (ROPE_THETA ** (jnp.arange(0, HALF_DIM, 2, dtype=F32) / HALF_DIM))
    pos = jnp.concatenate([jnp.tile(jnp.arange(SEQ), BATCH),
                           jnp.tile(PAST_LEN + jnp.arange(DEC_SEQ), DEC_BATCH)])
    ang = pos.astype(F32)[:, None] * inv[None, :]
    cos, sin = jnp.cos(ang), jnp.sin(ang)
    cos = jnp.tile(cos, (1, HEAD_DIM // (HALF_DIM // 2)))
    sin = jnp.tile(jnp.concatenate([-sin, sin], axis=1), (1, 2))
    return cos, sin


def _mod_rows(ada):
    ada = ada[:BATCH + DEC_BATCH].reshape(BATCH + DEC_BATCH, N_ADA, D_MODEL)
    prompt = jnp.broadcast_to(ada[:BATCH].transpose(1, 0, 2)[:, :, None, :],
                              (N_ADA, BATCH, ROW_TILE, D_MODEL))
    sample = jnp.repeat(ada[BATCH:], DEC_SEQ, axis=0)
    sample = sample.reshape(M_SAMPLE // ROW_TILE, ROW_TILE, N_ADA, D_MODEL).transpose(2, 0, 1, 3)
    return jnp.concatenate([prompt, sample], axis=1)


def kernel(x_prompt, x_sample, cache_k, cache_v, state_pool, page_table, c_prompt, c_sample,
           w_ada, b_ada, g_pre_ffn1, w1_ffn1, w3_ffn1, w2_ffn1, g_post_ffn1,
           g_pre_mix, w_in, lambda_q1, lambda_k1, lambda_q2, lambda_k2, subln_g,
           w_pool_mix, pool_scale, w_up_attn, w_up_pool, w_merge_gate, w_out, g_post_mix,
           g_pre_ffn2, w1_ffn2, w3_ffn2, w2_ffn2, g_post_ffn2):
    x = (x_prompt.reshape(M_PROMPT, D_MODEL), x_sample.reshape(M_SAMPLE, D_MODEL))
    c = jnp.concatenate([c_prompt, c_sample])
    c = jnp.pad(c, ((0, 48 - c.shape[0]), (0, 0)))
    mod = _mod_rows(_ada(c, w_ada, b_ada))
    lams = [v.reshape(1, HALF_DIM) for v in (lambda_q1, lambda_k1, lambda_q2, lambda_k2)]

    h = _norm_mod(x, g_pre_ffn1, mod, 0)
    f = _down(_swiglu_up(h, w1_ffn1, w3_ffn1), w2_ffn1)
    x, h = _resid_norm_mod(x, f, g_post_ffn1, g_pre_mix, mod, 0, 1, 0.5)

    tabs = _rope_tables()
    q_scale = HALF_DIM ** -0.5 * LOG2_E
    q_p = _proj(h, w_in, 0, ATTN_WIDTH, BF16, _PROMPT_ROWS, tabs, q_scale)
    k_p = _proj(h, w_in, ATTN_WIDTH, ATTN_WIDTH, F32, _PROMPT_ROWS, tabs)
    v_p = _proj(h, w_in, 2 * ATTN_WIDTH, ATTN_WIDTH, F32, _PROMPT_ROWS)
    u_p = _proj(h, w_in, 3 * ATTN_WIDTH, POOL_WIDTH, F32, _PROMPT_ROWS)
    q_s = _proj(h, w_in, 0, ATTN_WIDTH, F32, _SAMPLE_ROWS, tabs, q_scale)
    k_s = _proj(h, w_in, ATTN_WIDTH, ATTN_WIDTH, F32, _SAMPLE_ROWS, tabs)
    v_s = _proj(h, w_in, 2 * ATTN_WIDTH, ATTN_WIDTH, F32, _SAMPLE_ROWS)
    u_s = _proj(h, w_in, 3 * ATTN_WIDTH, POOL_WIDTH, F32, _SAMPLE_ROWS)

    attn = _prompt_attn(q_p, k_p, v_p, lams, subln_g)
    attn = _sample_attn(q_s, k_s, v_s, cache_k, cache_v, page_table, lams, subln_g, attn)

    u_sample = u_s.reshape(DEC_BATCH, DEC_SEQ, POOL_WIDTH)
    state16 = jnp.pad(state_pool, ((0, 0), (_HALO - POOL_BUF, 0), (0, 0)))
    pooled = _pool_prompt(u_p, w_pool_mix, pool_scale)
    pooled = _pool_sample(state16, u_sample, w_pool_mix, pool_scale, pooled)

    merged = _merge(h, attn, pooled, w_merge_gate, w_up_attn, w_up_pool)
    m = _proj(merged, w_out, 0, D_MODEL, F32, _ALL_ROWS)
    x, h = _resid_norm_mod(x, m, g_post_mix, g_pre_ffn2, mod, 1, 2, 1.0)

    f = _down(_swiglu_up(h, w1_ffn2, w3_ffn2), w2_ffn2)
    y_prompt, y_sample = _resid(x, f, g_post_ffn2, mod, 2, 0.5)
    y_prompt = y_prompt.reshape(BATCH, SEQ, D_MODEL)
    y_sample = y_sample.reshape(DEC_BATCH, DEC_SEQ, D_MODEL)
    k_prompt = k_p.reshape(BATCH, SEQ, N_HEADS, HEAD_DIM)
    v_prompt = v_p.reshape(BATCH, SEQ, N_HEADS, HEAD_DIM)
    k_sample = k_s.reshape(DEC_BATCH, DEC_SEQ, N_HEADS, HEAD_DIM)
    v_sample = v_s.reshape(DEC_BATCH, DEC_SEQ, N_HEADS, HEAD_DIM)
    pool_prompt = u_p.reshape(BATCH, SEQ, POOL_WIDTH)[:, SEQ - POOL_BUF:]
    pool_sample = jnp.concatenate([state_pool, u_sample], axis=1)[:, -POOL_BUF:]
    return (y_prompt, y_sample, k_prompt, v_prompt, pool_prompt, k_sample, v_sample, pool_sample)
```

```python
import functools
import math

import jax
import jax.numpy as jnp
from jax import lax
from jax.experimental import pallas as pl
from jax.experimental.pallas import tpu as pltpu

F32 = jnp.float32
BF16 = jnp.bfloat16

D_MODEL = 4096
BATCH = 2
SEQ = 4096
DEC_BATCH = 32
DEC_SEQ = 8
PAST_LEN = 8192
PAGE_SIZE = 128
N_PAGES = PAST_LEN // PAGE_SIZE
ATTN_WIDTH = D_MODEL // 2
HALF_DIM = 64
HEAD_DIM = 2 * HALF_DIM
N_HEADS = ATTN_WIDTH // HEAD_DIM
POOL_WIDTH = D_MODEL - ATTN_WIDTH
POOL_WINDOWS = (2, 4, 8, 16)
POOL_GROUP = POOL_WIDTH // len(POOL_WINDOWS)
POOL_BUF = max(POOL_WINDOWS) - 1
D_FF = ((8 * D_MODEL // 3 + 255) // 256) * 256
ROPE_THETA = 10000.0
N_ADA = 9
EPS = 1e-6
NEG = -1e30
LAM_INIT = 0.8 - 0.6 * math.exp(-0.3 * 0)

M_PROMPT = BATCH * SEQ
M_SAMPLE = DEC_BATCH * DEC_SEQ
M_ALL = M_PROMPT + M_SAMPLE

VMEM_LIMIT_BYTES = 58 * 1024 * 1024

ROW_TILE = 128
MM_ROWS = 768
STAGED_ROWS = 1056
MERGE_ROWS = 528
EPILOGUE_CHUNKS = 2
MERGE_CHUNKS = 3
LOG2_E = math.log2(math.e)
PAGES_PER_STEP = 8
HEADS_PER_GROUP = 8
Q_TILE = 512
Q_ROW_BLOCKS = 2


def _params(sem):
    return pltpu.CompilerParams(dimension_semantics=sem, vmem_limit_bytes=VMEM_LIMIT_BYTES)


def _sigmoid(x):
    return 1.0 / (1.0 + jnp.exp(-x))


def _rms(x, g):
    return x * lax.rsqrt(jnp.mean(x * x, axis=-1, keepdims=True) + EPS) * g


def _ada_kernel(c_ref, w_ref, b_ref, o_ref):
    c = c_ref[...]
    a = (c * _sigmoid(c)).astype(BF16)
    o_ref[...] = jnp.dot(a, w_ref[...].astype(BF16), preferred_element_type=F32) + b_ref[...]


def _ada(c_pad, w_ada, b_ada):
    rows = c_pad.shape[0]
    n = w_ada.shape[1]
    tn = 1024
    return pl.pallas_call(
        _ada_kernel,
        grid=(n // tn,),
        in_specs=[pl.BlockSpec((rows, D_MODEL), lambda j: (0, 0)),
                  pl.BlockSpec((D_MODEL, tn), lambda j: (0, j)),
                  pl.BlockSpec((1, tn), lambda j: (0, j))],
        out_specs=pl.BlockSpec((rows, tn), lambda j: (0, j)),
        out_shape=jax.ShapeDtypeStruct((rows, n), F32),
        compiler_params=_params(("arbitrary",)),
    )(c_pad, w_ada, b_ada.reshape(1, n))


_PROMPT_TILES = M_PROMPT // ROW_TILE
_TILES_PER_BATCH = SEQ // ROW_TILE


def _mod_group(i):
    return jnp.where(i < _PROMPT_TILES, i // _TILES_PER_BATCH, i - _PROMPT_TILES + BATCH)


def _mod_spec(j):
    return pl.BlockSpec((1, 1, ROW_TILE, D_MODEL), lambda i: (j, _mod_group(i), 0, 0))


def _row_spec():
    return pl.BlockSpec((ROW_TILE, D_MODEL), lambda i: (i, 0))


def _pair_specs():
    return [pl.BlockSpec((ROW_TILE, D_MODEL), lambda i: (jnp.minimum(i, _PROMPT_TILES - 1), 0)),
            pl.BlockSpec((ROW_TILE, D_MODEL), lambda i: (jnp.maximum(i - _PROMPT_TILES, 0), 0))]


def _pair_shapes():
    return [jax.ShapeDtypeStruct((M_PROMPT, D_MODEL), F32),
            jax.ShapeDtypeStruct((M_SAMPLE, D_MODEL), F32)]


def _vec_spec():
    return pl.BlockSpec((1, D_MODEL), lambda i: (0, 0))


def _read_pair(xp_ref, xs_ref):
    return jnp.where(pl.program_id(0) < _PROMPT_TILES, xp_ref[...], xs_ref[...])


def _write_pair(xp_ref, xs_ref, x):
    @pl.when(pl.program_id(0) < _PROMPT_TILES)
    def _():
        xp_ref[...] = x

    @pl.when(pl.program_id(0) >= _PROMPT_TILES)
    def _():
        xs_ref[...] = x


def _norm_mod_kernel(xp_ref, xs_ref, g_ref, scale_ref, shift_ref, h_ref):
    h = _rms(_read_pair(xp_ref, xs_ref), g_ref[...])
    h_ref[...] = (h * (1.0 + scale_ref[0, 0]) + shift_ref[0, 0]).astype(h_ref.dtype)


def _norm_mod(x, g, mod, j):
    return pl.pallas_call(
        _norm_mod_kernel,
        grid=(M_ALL // ROW_TILE,),
        in_specs=_pair_specs() + [_vec_spec(), _mod_spec(3 * j + 1), _mod_spec(3 * j)],
        out_specs=_row_spec(),
        out_shape=jax.ShapeDtypeStruct((M_ALL, D_MODEL), BF16),
        compiler_params=_params(("arbitrary",)),
    )(*x, g.reshape(1, D_MODEL), mod, mod)


def _resid_norm_mod_kernel(xp_ref, xs_ref, f_ref, gpost_ref, gate_ref, gpre_ref, scale_ref,
                           shift_ref, xpo_ref, xso_ref, h_ref, *, coef):
    x = _read_pair(xp_ref, xs_ref) + coef * gate_ref[0, 0] * _rms(f_ref[...], gpost_ref[...])
    _write_pair(xpo_ref, xso_ref, x)
    h = _rms(x, gpre_ref[...])
    h_ref[...] = (h * (1.0 + scale_ref[0, 0]) + shift_ref[0, 0]).astype(h_ref.dtype)


def _resid_norm_mod(x, f, gpost, gpre, mod, j_prev, j_next, coef):
    xp, xs, h = pl.pallas_call(
        functools.partial(_resid_norm_mod_kernel, coef=coef),
        grid=(M_ALL // ROW_TILE,),
        in_specs=_pair_specs() + [_row_spec(), _vec_spec(), _mod_spec(3 * j_prev + 2), _vec_spec(),
                                  _mod_spec(3 * j_next + 1), _mod_spec(3 * j_next)],
        out_specs=_pair_specs() + [_row_spec()],
        out_shape=_pair_shapes() + [jax.ShapeDtypeStruct((M_ALL, D_MODEL), BF16)],
        compiler_params=_params(("arbitrary",)),
    )(*x, f, gpost.reshape(1, D_MODEL), mod, gpre.reshape(1, D_MODEL), mod, mod)
    return (xp, xs), h


def _resid_kernel(xp_ref, xs_ref, f_ref, gpost_ref, gate_ref, xpo_ref, xso_ref, *, coef):
    x = _read_pair(xp_ref, xs_ref) + coef * gate_ref[0, 0] * _rms(f_ref[...], gpost_ref[...])
    _write_pair(xpo_ref, xso_ref, x)


def _resid(x, f, gpost, mod, j_prev, coef):
    return pl.pallas_call(
        functools.partial(_resid_kernel, coef=coef),
        grid=(M_ALL // ROW_TILE,),
        in_specs=_pair_specs() + [_row_spec(), _vec_spec(), _mod_spec(3 * j_prev + 2)],
        out_specs=_pair_specs(),
        out_shape=_pair_shapes(),
        compiler_params=_params(("arbitrary",)),
    )(*x, f, gpost.reshape(1, D_MODEL), mod)


def _cast_kernel(w_ref, o_ref):
    o_ref[...] = w_ref[...].astype(o_ref.dtype)


def _first_tile(w, col0, tn):
    k = w.shape[0]
    joff = col0 // tn
    return pl.pallas_call(
        _cast_kernel,
        grid=(1,),
        in_specs=[pl.BlockSpec((k, tn), lambda i: (0, joff))],
        out_specs=pl.BlockSpec((k, tn), lambda i: (0, 0)),
        out_shape=jax.ShapeDtypeStruct((k, tn), BF16),
        compiler_params=_params(("arbitrary",)),
    )(w)


def _chunk_spec(w, col0, tn, nj, ni):
    k = w.shape[0]
    assert k % ni == 0 and (k // ni) % 8 == 0 and col0 % tn == 0
    joff = col0 // tn
    return pl.BlockSpec((k // ni, tn), lambda j, i: (i, jnp.minimum(j + 1, nj - 1) + joff))


def _stage_weights(chunk_refs, first_refs, slot_refs):
    j, i = pl.program_id(0), pl.program_id(1)

    @pl.when((j == 0) & (i == 0))
    def _():
        for first, slots in zip(first_refs, slot_refs):
            pltpu.sync_copy(first, slots.at[0])

    @pl.when(j + 1 < pl.num_programs(0))
    def _():
        nxt = (j + 1) % 2
        for chunk, slots in zip(chunk_refs, slot_refs):
            ck = chunk.shape[0]
            slots[nxt, pl.ds(pl.multiple_of(i * ck, ck), ck), :] = chunk[...].astype(BF16)

    return j % 2


def _slot_scratch(w, tn):
    return pltpu.VMEM((2, w.shape[0], tn), BF16)


_HBM = pl.BlockSpec(memory_space=pl.ANY)


def _swiglu_up_kernel(a_ref, w1c, w3c, w1f, w3f, o_ref, w1q, w3q):
    slot = _stage_weights((w1c, w3c), (w1f, w3f), (w1q, w3q))
    rows = a_ref.shape[0] // EPILOGUE_CHUNKS
    for c in range(EPILOGUE_CHUNKS):
        sl = pl.ds(c * rows, rows)
        a = a_ref[sl, :]
        u = jnp.dot(a, w1q[slot], preferred_element_type=F32)
        v = jnp.dot(a, w3q[slot], preferred_element_type=F32)
        o_ref[sl, :] = (u * _sigmoid(u) * v).astype(o_ref.dtype)


def _swiglu_cols(h, w1, w3, col0, n, tn):
    tm = STAGED_ROWS
    ni, nj = M_ALL // tm, n // tn
    assert n % tn == 0 and M_ALL % tm == 0
    return pl.pallas_call(
        _swiglu_up_kernel,
        grid=(nj, ni),
        in_specs=[pl.BlockSpec((tm, D_MODEL), lambda j, i: (i, 0)),
                  _chunk_spec(w1, col0, tn, nj, ni), _chunk_spec(w3, col0, tn, nj, ni),
                  _HBM, _HBM],
        out_specs=pl.BlockSpec((tm, tn), lambda j, i: (i, j)),
        out_shape=jax.ShapeDtypeStruct((M_ALL, n), BF16),
        scratch_shapes=[_slot_scratch(w1, tn), _slot_scratch(w3, tn)],
        compiler_params=_params(("arbitrary", "arbitrary")),
    )(h, w1, w3, _first_tile(w1, col0, tn), _first_tile(w3, col0, tn))


def _swiglu_up(h, w1, w3):
    wide = (D_FF // 512) * 512
    return (_swiglu_cols(h, w1, w3, 0, wide, 512),
            _swiglu_cols(h, w1, w3, wide, D_FF - wide, D_FF - wide))


def _down_kernel(a0_ref, a1_ref, at_ref, w0_ref, w1_ref, o_ref):
    k = pl.program_id(1)
    last = pl.num_programs(1) - 1

    def first(a_ref):
        return jnp.dot(a_ref[...], w0_ref[...].astype(BF16), preferred_element_type=F32)

    def second():
        return jnp.dot(a1_ref[...], w1_ref[...].astype(BF16), preferred_element_type=F32)

    @pl.when(k == 0)
    def _():
        o_ref[...] = first(a0_ref) + second()

    @pl.when((k != 0) & (k != last))
    def _():
        o_ref[...] += first(a0_ref) + second()

    @pl.when(k == last)
    def _():
        o_ref[...] += first(at_ref)


def _down(g, w2):
    g_wide, g_tail = g
    tk = g_tail.shape[1]
    tm = MM_ROWS
    nk = D_FF // tk
    nw = g_wide.shape[1] // tk
    assert nk % 2 == 1 and nw == nk - 1
    return pl.pallas_call(
        _down_kernel,
        grid=(M_ALL // tm, (nk + 1) // 2),
        in_specs=[pl.BlockSpec((tm, tk), lambda i, k: (i, jnp.minimum(2 * k, nw - 2))),
                  pl.BlockSpec((tm, tk), lambda i, k: (i, jnp.minimum(2 * k + 1, nw - 1))),
                  pl.BlockSpec((tm, tk), lambda i, k: (i, 0)),
                  pl.BlockSpec((tk, D_MODEL), lambda i, k: (2 * k, 0)),
                  pl.BlockSpec((tk, D_MODEL), lambda i, k: (jnp.minimum(2 * k + 1, nk - 1), 0))],
        out_specs=pl.BlockSpec((tm, D_MODEL), lambda i, k: (i, 0)),
        out_shape=jax.ShapeDtypeStruct((M_ALL, D_MODEL), F32),
        compiler_params=_params(("arbitrary", "arbitrary")),
    )(g_wide, g_wide, g_tail, w2, w2)


def _proj_kernel(a_ref, wc, wf, *rest, rope, out_scale):
    if rope:
        cos_ref, sin_ref, o_ref, wq = rest
    else:
        o_ref, wq = rest
    slot = _stage_weights((wc,), (wf,), (wq,))
    rows = a_ref.shape[0] // EPILOGUE_CHUNKS
    tn = o_ref.shape[1]
    for c in range(EPILOGUE_CHUNKS):
        sl = pl.ds(c * rows, rows)
        y = jnp.dot(a_ref[sl, :], wq[slot], preferred_element_type=F32)
        if rope:
            reps = tn // HEAD_DIM
            cos = jnp.tile(cos_ref[sl, :], (1, reps))
            sin = jnp.tile(sin_ref[sl, :], (1, reps))
            lane = lax.broadcasted_iota(jnp.int32, y.shape, 1)
            first = (lane % HALF_DIM) < (HALF_DIM // 2)
            partner = jnp.where(first, pltpu.roll(y, tn - HALF_DIM // 2, 1),
                                pltpu.roll(y, HALF_DIM // 2, 1))
            y = y * cos + partner * sin
        if out_scale != 1.0:
            y = y * out_scale
        o_ref[sl, :] = y.astype(o_ref.dtype)


def _proj(a, w, col0, n, out_dtype, rows, rope_tabs=None, out_scale=1.0):
    row0, m, tm, tn = rows
    k = a.shape[1]
    ni, nj = m // tm, n // tn
    ioff = row0 // tm
    assert row0 % tm == 0 and m % tm == 0 and n % tn == 0
    in_specs = [pl.BlockSpec((tm, k), lambda j, i: (i + ioff, 0)),
                _chunk_spec(w, col0, tn, nj, ni), _HBM]
    args = [a, w, _first_tile(w, col0, tn)]
    if rope_tabs is not None:
        in_specs += [pl.BlockSpec((tm, HEAD_DIM), lambda j, i: (i + ioff, 0))] * 2
        args += list(rope_tabs)
    return pl.pallas_call(
        functools.partial(_proj_kernel, rope=rope_tabs is not None, out_scale=out_scale),
        grid=(nj, ni),
        in_specs=in_specs,
        out_specs=pl.BlockSpec((tm, tn), lambda j, i: (i, j)),
        out_shape=jax.ShapeDtypeStruct((m, n), out_dtype),
        scratch_shapes=[_slot_scratch(w, tn)],
        compiler_params=_params(("arbitrary", "arbitrary")),
    )(*args)


_ALL_ROWS = (0, M_ALL, STAGED_ROWS, 1024)
_PROMPT_ROWS = (0, M_PROMPT, 1024, 1024)
_SAMPLE_ROWS = (M_PROMPT, M_SAMPLE, M_SAMPLE, 512)


def _merge_kernel(h_ref, at_ref, po_ref, wg0c, wg1c, wac, wpc, wg0f, wg1f, waf, wpf, o_ref,
                  wg0q, wg1q, waq, wpq):
    slot = _stage_weights((wg0c, wg1c, wac, wpc), (wg0f, wg1f, waf, wpf), (wg0q, wg1q, waq, wpq))
    rows = h_ref.shape[0] // MERGE_CHUNKS
    for c in range(MERGE_CHUNKS):
        sl = pl.ds(c * rows, rows)
        h = h_ref[sl, :]
        g0 = _sigmoid(jnp.dot(h, wg0q[slot], preferred_element_type=F32))
        g1 = _sigmoid(jnp.dot(h, wg1q[slot], preferred_element_type=F32))
        a_up = jnp.dot(at_ref[sl, :], waq[slot], preferred_element_type=F32)
        p_up = jnp.dot(po_ref[sl, :], wpq[slot], preferred_element_type=F32)
        o_ref[sl, :] = (g0 * a_up + g1 * p_up).astype(o_ref.dtype)


def _merge(h, attn, pooled, w_gate, w_up_attn, w_up_pool):
    tn = 512
    tm = MERGE_ROWS
    ni, nj = M_ALL // tm, D_MODEL // tn
    cols = [(w_gate, 0), (w_gate, D_MODEL), (w_up_attn, 0), (w_up_pool, 0)]
    return pl.pallas_call(
        _merge_kernel,
        grid=(nj, ni),
        in_specs=[pl.BlockSpec((tm, D_MODEL), lambda j, i: (i, 0)),
                  pl.BlockSpec((tm, ATTN_WIDTH), lambda j, i: (i, 0)),
                  pl.BlockSpec((tm, POOL_WIDTH), lambda j, i: (i, 0))]
                 + [_chunk_spec(w, c0, tn, nj, ni) for w, c0 in cols] + [_HBM] * len(cols),
        out_specs=pl.BlockSpec((tm, tn), lambda j, i: (i, j)),
        out_shape=jax.ShapeDtypeStruct((M_ALL, D_MODEL), BF16),
        scratch_shapes=[_slot_scratch(w, tn) for w, _ in cols],
        compiler_params=_params(("arbitrary", "arbitrary")),
    )(h, attn, pooled, *[w for w, _ in cols], *[_first_tile(w, c0, tn) for w, c0 in cols])


def _lambda(lq1, lk1, lq2, lk2):
    return (jnp.exp(jnp.sum(lq1[...] * lk1[...], axis=-1, keepdims=True))
            - jnp.exp(jnp.sum(lq2[...] * lk2[...], axis=-1, keepdims=True)) + LAM_INIT)


def _split_halves(q):
    lane = lax.broadcasted_iota(jnp.int32, q.shape, 1)
    zero = jnp.zeros_like(q)
    return jnp.concatenate([jnp.where(lane < HALF_DIM, q, zero),
                            jnp.where(lane >= HALF_DIM, q, zero)], axis=0)


def _head_out(o, lam, g):
    t = o.shape[0] // 2
    a = o[:t] - lam * o[t:]
    return _rms(a, g) * (1.0 - LAM_INIT)


def _online_update(m_ref, acc_ref, idx, s, v_ones):
    m_prev = m_ref[idx]
    m_new = jnp.maximum(m_prev, jnp.max(s, axis=-1, keepdims=True))
    p = jnp.exp2(s - jnp.tile(m_new, (1, s.shape[1] // HEAD_DIM)))
    alpha = jnp.exp2(m_prev - m_new)
    acc_ref[idx] = (acc_ref[idx] * jnp.tile(alpha, (1, 2))
                    + jnp.dot(p.astype(BF16), v_ones, preferred_element_type=F32))
    m_ref[idx] = m_new


def _prompt_attn_kernel(q_ref, k_ref, v_ref, lq1, lk1, lq2, lk2, g_ref, init_ref, o_ref,
                        kb, vb, m_s, acc_s):
    del init_ref
    tq = Q_TILE
    kb[...] = k_ref[...].astype(BF16)
    vb[:, :HEAD_DIM] = v_ref[...].astype(BF16)
    vb[:, HEAD_DIM:] = jnp.ones((SEQ, HEAD_DIM), BF16)
    lam = _lambda(lq1, lk1, lq2, lk2)
    g = g_ref[...]

    n_chains, rows = m_s.shape[0], m_s.shape[1]
    row_blocks = n_chains // 2

    def q_tile(qi, carry):
        q0 = pl.multiple_of(qi * tq, tq)
        chains = []
        for c in range(n_chains):
            r0 = (c % row_blocks) * rows
            q = q_ref[pl.ds(q0 + r0, rows), :]
            lane = lax.broadcasted_iota(jnp.int32, q.shape, 1)
            keep = (lane < HALF_DIM) if c < row_blocks else (lane >= HALF_DIM)
            chains.append((r0, jnp.where(keep, q, jnp.zeros_like(q))))
        m_s[...] = jnp.full(m_s.shape, NEG, F32)
        acc_s[...] = jnp.zeros(acc_s.shape, F32)

        def step(j, masked):
            start = pl.multiple_of(j * tq, tq)
            k = kb[pl.ds(start, tq), :]
            v = vb[pl.ds(start, tq), :]
            for c, (r0, qc) in enumerate(chains):
                nk = r0 + rows if masked else tq
                s = lax.dot_general(qc, k[:nk], (((1,), (1,)), ((), ())),
                                    preferred_element_type=F32)
                if masked:
                    row = r0 + lax.broadcasted_iota(jnp.int32, s.shape, 0)
                    col = lax.broadcasted_iota(jnp.int32, s.shape, 1)
                    s = jnp.where(col <= row, s, NEG)
                _online_update(m_s, acc_s, (c,), s, v[:nk])

        def body(j, inner):
            step(j, False)
            return inner

        lax.fori_loop(0, qi, body, 0)
        step(qi, True)

        for rb in range(row_blocks):
            acc = jnp.concatenate([acc_s[rb], acc_s[row_blocks + rb]], axis=0)
            o = acc[:, :HEAD_DIM] / acc[:, HEAD_DIM:]
            o_ref[pl.ds(q0 + rb * rows, rows), :] = _head_out(o, lam, g).astype(o_ref.dtype)
        return carry

    lax.fori_loop(0, SEQ // tq, q_tile, 0)


def _prompt_attn(q, k, v, lams, subln_g):
    tq = Q_TILE
    small = pl.BlockSpec((1, HALF_DIM), lambda b, h: (0, 0))
    head = pl.BlockSpec((SEQ, HEAD_DIM), lambda b, h: (b, h))
    return pl.pallas_call(
        _prompt_attn_kernel,
        grid=(BATCH, N_HEADS),
        in_specs=[head, head, head, small, small, small, small,
                  pl.BlockSpec((1, HEAD_DIM), lambda b, h: (0, 0)), _HBM],
        out_specs=head,
        out_shape=jax.ShapeDtypeStruct((M_ALL, ATTN_WIDTH), BF16),
        input_output_aliases={8: 0},
        scratch_shapes=[pltpu.VMEM((SEQ, HEAD_DIM), BF16), pltpu.VMEM((SEQ, 2 * HEAD_DIM), BF16),
                        pltpu.VMEM((2 * Q_ROW_BLOCKS, tq // Q_ROW_BLOCKS, HEAD_DIM), F32),
                        pltpu.VMEM((2 * Q_ROW_BLOCKS, tq // Q_ROW_BLOCKS, 2 * HEAD_DIM), F32)],
        compiler_params=_params(("arbitrary", "arbitrary")),
    )(q, k, v, *lams, subln_g.reshape(1, HEAD_DIM), jnp.zeros((M_ALL, ATTN_WIDTH), BF16))


_GROUP_ROWS = HEADS_PER_GROUP * 2 * DEC_SEQ
_SLAB_ROWS = PAGE_SIZE * HEADS_PER_GROUP
_N_GROUPS = N_HEADS // HEADS_PER_GROUP


def _sample_attn_kernel(pt_ref, q_ref, kn_ref, vn_ref, *rest):
    del pt_ref
    kpages = rest[:PAGES_PER_STEP]
    vpages = rest[PAGES_PER_STEP:2 * PAGES_PER_STEP]
    (lq1, lk1, lq2, lk2, g_ref, attn_in, o_ref,
     qz_s, bias_s, m_s, acc_s, out_s) = rest[2 * PAGES_PER_STEP:]
    del attn_in
    b = pl.program_id(0)
    step = pl.program_id(1)
    last_step = pl.num_programs(1) - 1

    @pl.when(step == 0)
    def _():
        for h in range(N_HEADS):
            grp, hl = divmod(h, HEADS_PER_GROUP)
            qh = q_ref[:, h * HEAD_DIM:(h + 1) * HEAD_DIM]
            qz_s[grp, hl * 2 * DEC_SEQ:(hl + 1) * 2 * DEC_SEQ, :] = _split_halves(qh)
        m_s[...] = jnp.full(m_s.shape, NEG, F32)
        acc_s[...] = jnp.zeros(acc_s.shape, F32)

    @pl.when((b == 0) & (step == 0))
    def _():
        row = lax.broadcasted_iota(jnp.int32, bias_s.shape, 0)
        col = lax.broadcasted_iota(jnp.int32, bias_s.shape, 1)
        bias_s[...] = jnp.where(col % HEADS_PER_GROUP == row // (2 * DEC_SEQ), 0.0, NEG)

    ones = jnp.ones((PAGES_PER_STEP * _SLAB_ROWS, HEAD_DIM), BF16)
    for grp in range(_N_GROUPS):
        lo = grp * HEADS_PER_GROUP
        qz = qz_s[grp].astype(BF16)
        k2 = jnp.concatenate(
            [kp[0, :, lo:lo + HEADS_PER_GROUP, :].reshape(_SLAB_ROWS, HEAD_DIM).astype(BF16)
             for kp in kpages], axis=0)
        v2 = jnp.concatenate(
            [vp[0, :, lo:lo + HEADS_PER_GROUP, :].reshape(_SLAB_ROWS, HEAD_DIM).astype(BF16)
             for vp in vpages], axis=0)
        s = lax.dot_general(qz, k2, (((1,), (1,)), ((), ())), preferred_element_type=F32)
        s = s + bias_s[...]
        _online_update(m_s, acc_s, (grp,), s, jnp.concatenate([v2, ones], axis=1))

    @pl.when(step == last_step)
    def _():
        lam = _lambda(lq1, lk1, lq2, lk2)
        g = g_ref[...]
        out_rows = pl.ds(pl.multiple_of(b * DEC_SEQ, DEC_SEQ), DEC_SEQ)
        for h in range(N_HEADS):
            grp, hl = divmod(h, HEADS_PER_GROUP)
            rows = pl.ds(hl * 2 * DEC_SEQ, 2 * DEC_SEQ)
            cols = slice(h * HEAD_DIM, (h + 1) * HEAD_DIM)
            qh = qz_s[grp, rows, :]
            s = lax.dot_general(qh, kn_ref[:, cols], (((1,), (1,)), ((), ())),
                                preferred_element_type=F32)
            row = lax.broadcasted_iota(jnp.int32, s.shape, 0)
            col = lax.broadcasted_iota(jnp.int32, s.shape, 1)
            causal = col <= row % DEC_SEQ
            s = jnp.where(causal, s, NEG)
            m_prev = m_s[grp, rows, :]
            m_new = jnp.maximum(m_prev, jnp.max(s, axis=-1, keepdims=True))
            p = jnp.where(causal, jnp.exp2(s - m_new[:, :DEC_SEQ]), 0.0)
            alpha = jnp.exp2(m_prev - m_new)
            acc = acc_s[grp, rows, :]
            num = acc[:, :HEAD_DIM] * alpha + jnp.dot(p, vn_ref[:, cols],
                                                      preferred_element_type=F32)
            den = acc[:, HEAD_DIM:] * alpha + jnp.sum(p, axis=-1, keepdims=True)
            out_s[out_rows, cols] = _head_out(num / den, lam, g)

    @pl.when((b == pl.num_programs(0) - 1) & (step == last_step))
    def _():
        o_ref[...] = out_s[...].astype(o_ref.dtype)


def _sample_attn(q, k, v, cache_k, cache_v, page_table, lams, subln_g, attn_all):
    pps = PAGES_PER_STEP

    def new_spec():
        return pl.BlockSpec((DEC_SEQ, ATTN_WIDTH), lambda b, s, pt: (b, 0))

    def page_spec(p):
        return pl.BlockSpec((1, PAGE_SIZE, N_HEADS, HEAD_DIM),
                            lambda b, s, pt: (pt[b, s * pps + p], 0, 0, 0))

    small = pl.BlockSpec((1, HALF_DIM), lambda b, s, pt: (0, 0))
    in_specs = ([new_spec(), new_spec(), new_spec()]
                + [page_spec(p) for p in range(pps)] + [page_spec(p) for p in range(pps)]
                + [small, small, small, small,
                   pl.BlockSpec((1, HEAD_DIM), lambda b, s, pt: (0, 0)),
                   pl.BlockSpec(memory_space=pl.ANY)])
    grid_spec = pltpu.PrefetchScalarGridSpec(
        num_scalar_prefetch=1,
        grid=(DEC_BATCH, N_PAGES // pps),
        in_specs=in_specs,
        out_specs=pl.BlockSpec((M_SAMPLE, ATTN_WIDTH),
                               lambda b, s, pt: (M_PROMPT // M_SAMPLE, 0)),
        scratch_shapes=[pltpu.VMEM((_N_GROUPS, _GROUP_ROWS, HEAD_DIM), F32),
                        pltpu.VMEM((_GROUP_ROWS, pps * _SLAB_ROWS), F32),
                        pltpu.VMEM((_N_GROUPS, _GROUP_ROWS, HEAD_DIM), F32),
                        pltpu.VMEM((_N_GROUPS, _GROUP_ROWS, 2 * HEAD_DIM), F32),
                        pltpu.VMEM((M_SAMPLE, ATTN_WIDTH), F32)],
    )
    return pl.pallas_call(
        _sample_attn_kernel,
        grid_spec=grid_spec,
        out_shape=jax.ShapeDtypeStruct((M_ALL, ATTN_WIDTH), BF16),
        input_output_aliases={len(in_specs): 0},
        compiler_params=_params(("arbitrary", "arbitrary")),
    )(page_table, q, k, v, *([cache_k] * pps), *([cache_v] * pps), *lams,
      subln_g.reshape(1, HEAD_DIM), attn_all)


_HALO = 16


def _pool_groups(ext_ref, lead, t, pos, write):
    for g, w in enumerate(POOL_WINDOWS):
        cols = slice(g * POOL_GROUP, (g + 1) * POOL_GROUP)
        tot = None
        for s in range(w):
            part = ext_ref[lead + (pl.ds(_HALO - s, t), cols)]
            tot = part if tot is None else tot + part
        cnt = jnp.minimum(pos + 1, w).astype(F32)
        pooled = tot / cnt - ext_ref[lead + (pl.ds(_HALO, t), cols)]
        write(g, cols, pooled)


def _pool_prompt_kernel(u_ref, w_ref, scale_ref, init_ref, o_ref, ext_s, wq_s):
    del init_ref
    i = pl.program_id(1)
    t = u_ref.shape[0]

    @pl.when((pl.program_id(0) == 0) & (i == 0))
    def _():
        wq_s[...] = w_ref[...].astype(BF16)

    @pl.when(i == 0)
    def _():
        ext_s[0:_HALO, :] = jnp.zeros((_HALO, POOL_WIDTH), F32)

    @pl.when(i != 0)
    def _():
        ext_s[0:_HALO, :] = ext_s[t:t + _HALO, :]

    ext_s[_HALO:_HALO + t, :] = u_ref[...]
    pos = i * t + lax.broadcasted_iota(jnp.int32, (t, 1), 0)

    def write(g, cols, pooled):
        y = jnp.dot(pooled.astype(BF16), wq_s[g], preferred_element_type=F32)
        o_ref[:, cols] = (y * scale_ref[:, cols]).astype(o_ref.dtype)

    _pool_groups(ext_s, (), t, pos, write)


def _pool_prompt(u, w_pool_mix, pool_scale):
    t = 256
    nt = SEQ // t
    return pl.pallas_call(
        _pool_prompt_kernel,
        grid=(BATCH, nt),
        in_specs=[pl.BlockSpec((t, POOL_WIDTH), lambda b, i: (b * nt + i, 0)),
                  pl.BlockSpec((len(POOL_WINDOWS), POOL_GROUP, POOL_GROUP), lambda b, i: (0, 0, 0)),
                  pl.BlockSpec((1, POOL_WIDTH), lambda b, i: (0, 0)), _HBM],
        out_specs=pl.BlockSpec((t, POOL_WIDTH), lambda b, i: (b * nt + i, 0)),
        out_shape=jax.ShapeDtypeStruct((M_ALL, POOL_WIDTH), BF16),
        input_output_aliases={3: 0},
        scratch_shapes=[pltpu.VMEM((_HALO + t, POOL_WIDTH), F32),
                        pltpu.VMEM((len(POOL_WINDOWS), POOL_GROUP, POOL_GROUP), BF16)],
        compiler_params=_params(("arbitrary", "arbitrary")),
    )(u, w_pool_mix, pool_scale.reshape(1, POOL_WIDTH), jnp.zeros((M_ALL, POOL_WIDTH), BF16))


def _pool_sample_kernel(state_ref, u_ref, w_ref, scale_ref, pooled_in, o_ref, ext_s):
    del pooled_in
    ext_s[:, 0:_HALO, :] = state_ref[...]
    ext_s[:, _HALO:_HALO + DEC_SEQ, :] = u_ref[...]
    pos = PAST_LEN + lax.broadcasted_iota(jnp.int32, (1, DEC_SEQ, 1), 1)

    def write(g, cols, pooled):
        x = pooled.reshape(M_SAMPLE, POOL_GROUP).astype(BF16)
        y = jnp.dot(x, w_ref[g].astype(BF16), preferred_element_type=F32)
        o_ref[:, cols] = (y * scale_ref[:, cols]).astype(o_ref.dtype)

    _pool_groups(ext_s, (slice(None),), DEC_SEQ, pos, write)


def _pool_sample(state16, u3, w_pool_mix, pool_scale, pooled_all):
    ng = len(POOL_WINDOWS)
    return pl.pallas_call(
        _pool_sample_kernel,
        grid=(1,),
        in_specs=[pl.BlockSpec((DEC_BATCH, _HALO, POOL_WIDTH), lambda i: (0, 0, 0)),
                  pl.BlockSpec((DEC_BATCH, DEC_SEQ, POOL_WIDTH), lambda i: (0, 0, 0)),
                  pl.BlockSpec((ng, POOL_GROUP, POOL_GROUP), lambda i: (0, 0, 0)),
                  pl.BlockSpec((1, POOL_WIDTH), lambda i: (0, 0)),
                  pl.BlockSpec(memory_space=pl.ANY)],
        out_specs=pl.BlockSpec((M_SAMPLE, POOL_WIDTH), lambda i: (M_PROMPT // M_SAMPLE, 0)),
        out_shape=jax.ShapeDtypeStruct((M_ALL, POOL_WIDTH), BF16),
        input_output_aliases={4: 0},
        scratch_shapes=[pltpu.VMEM((DEC_BATCH, _HALO + DEC_SEQ, POOL_WIDTH), F32)],
        compiler_params=_params(("arbitrary",)),
    )(state16, u3, w_pool_mix, pool_scale.reshape(1, POOL_WIDTH), pooled_all)


def _rope_tables():
    inv = 1.0 / (ROPE_THETA ** (jnp.arange(0, HALF_DIM, 2, dtype=F32) / HALF_DIM))
    pos = jnp.concatenate([jnp.tile(jnp.arange(SEQ), BATCH),
                           jnp.tile(PAST_LEN + jnp.arange(DEC_SEQ), DEC_BATCH)])
    ang = pos.astype(F32)[:, None] * inv[None, :]
    cos, sin = jnp.cos(ang), jnp.sin(ang)
    cos = jnp.tile(cos, (1, HEAD_DIM // (HALF_DIM // 2)))
    sin = jnp.tile(jnp.concatenate([-sin, sin], axis=1), (1, 2))
    return cos, sin


def _mod_rows(ada):
    ada = ada[:BATCH + DEC_BATCH].reshape(BATCH + DEC_BATCH, N_ADA, D_MODEL)
    prompt = jnp.broadcast_to(ada[:BATCH].transpose(1, 0, 2)[:, :, None, :],
                              (N_ADA, BATCH, ROW_TILE, D_MODEL))
    sample = jnp.repeat(ada[BATCH:], DEC_SEQ, axis=0)
    sample = sample.reshape(M_SAMPLE // ROW_TILE, ROW_TILE, N_ADA, D_MODEL).transpose(2, 0, 1, 3)
    return jnp.concatenate([prompt, sample], axis=1)


def kernel(x_prompt, x_sample, cache_k, cache_v, state_pool, page_table, c_prompt, c_sample,
           w_ada, b_ada, g_pre_ffn1, w1_ffn1, w3_ffn1, w2_ffn1, g_post_ffn1,
           g_pre_mix, w_in, lambda_q1, lambda_k1, lambda_q2, lambda_k2, subln_g,
           w_pool_mix, pool_scale, w_up_attn, w_up_pool, w_merge_gate, w_out, g_post_mix,
           g_pre_ffn2, w1_ffn2, w3_ffn2, w2_ffn2, g_post_ffn2):
    x = (x_prompt.reshape(M_PROMPT, D_MODEL), x_sample.reshape(M_SAMPLE, D_MODEL))
    c = jnp.concatenate([c_prompt, c_sample])
    c = jnp.pad(c, ((0, 48 - c.shape[0]), (0, 0)))
    mod = _mod_rows(_ada(c, w_ada, b_ada))
    lams = [v.reshape(1, HALF_DIM) for v in (lambda_q1, lambda_k1, lambda_q2, lambda_k2)]

    h = _norm_mod(x, g_pre_ffn1, mod, 0)
    f = _down(_swiglu_up(h, w1_ffn1, w3_ffn1), w2_ffn1)
    x, h = _resid_norm_mod(x, f, g_post_ffn1, g_pre_mix, mod, 0, 1, 0.5)

    tabs = _rope_tables()
    q_scale = HALF_DIM ** -0.5 * LOG2_E
    q_p = _proj(h, w_in, 0, ATTN_WIDTH, BF16, _PROMPT_ROWS, tabs, q_scale)
    k_p = _proj(h, w_in, ATTN_WIDTH, ATTN_WIDTH, F32, _PROMPT_ROWS, tabs)
    v_p = _proj(h, w_in, 2 * ATTN_WIDTH, ATTN_WIDTH, F32, _PROMPT_ROWS)
    u_p = _proj(h, w_in, 3 * ATTN_WIDTH, POOL_WIDTH, F32, _PROMPT_ROWS)
    q_s = _proj(h, w_in, 0, ATTN_WIDTH, F32, _SAMPLE_ROWS, tabs, q_scale)
    k_s = _proj(h, w_in, ATTN_WIDTH, ATTN_WIDTH, F32, _SAMPLE_ROWS, tabs)
    v_s = _proj(h, w_in, 2 * ATTN_WIDTH, ATTN_WIDTH, F32, _SAMPLE_ROWS)
    u_s = _proj(h, w_in, 3 * ATTN_WIDTH, POOL_WIDTH, F32, _SAMPLE_ROWS)

    attn = _prompt_attn(q_p, k_p, v_p, lams, subln_g)
    attn = _sample_attn(q_s, k_s, v_s, cache_k, cache_v, page_table, lams, subln_g, attn)

    u_sample = u_s.reshape(DEC_BATCH, DEC_SEQ, POOL_WIDTH)
    state16 = jnp.pad(state_pool, ((0, 0), (_HALO - POOL_BUF, 0), (0, 0)))
    pooled = _pool_prompt(u_p, w_pool_mix, pool_scale)
    pooled = _pool_sample(state16, u_sample, w_pool_mix, pool_scale, pooled)

    merged = _merge(h, attn, pooled, w_merge_gate, w_up_attn, w_up_pool)
    m = _proj(merged, w_out, 0, D_MODEL, F32, _ALL_ROWS)
    x, h = _resid_norm_mod(x, m, g_post_mix, g_pre_ffn2, mod, 1, 2, 1.0)

    f = _down(_swiglu_up(h, w1_ffn2, w3_ffn2), w2_ffn2)
    y_prompt, y_sample = _resid(x, f, g_post_ffn2, mod, 2, 0.5)
    y_prompt = y_prompt.reshape(BATCH, SEQ, D_MODEL)
    y_sample = y_sample.reshape(DEC_BATCH, DEC_SEQ, D_MODEL)
    k_prompt = k_p.reshape(BATCH, SEQ, N_HEADS, HEAD_DIM)
    v_prompt = v_p.reshape(BATCH, SEQ, N_HEADS, HEAD_DIM)
    k_sample = k_s.reshape(DEC_BATCH, DEC_SEQ, N_HEADS, HEAD_DIM)
    v_sample = v_s.reshape(DEC_BATCH, DEC_SEQ, N_HEADS, HEAD_DIM)
    pool_prompt = u_p.reshape(BATCH, SEQ, POOL_WIDTH)[:, SEQ - POOL_BUF:]
    pool_sample = jnp.concatenate([state_pool, u_sample], axis=1)[:, -POOL_BUF:]
    return (y_prompt, y_sample, k_prompt, v_prompt, pool_prompt, k_sample, v_sample, pool_sample)
```

```python
import functools
import math

import jax
import jax.numpy as jnp
from jax import lax
from jax.experimental import pallas as pl
from jax.experimental.pallas import tpu as pltpu

F32 = jnp.float32
BF16 = jnp.bfloat16

D_MODEL = 4096
BATCH = 2
SEQ = 4096
DEC_BATCH = 32
DEC_SEQ = 8
PAST_LEN = 8192
PAGE_SIZE = 128
N_PAGES = PAST_LEN // PAGE_SIZE
ATTN_WIDTH = D_MODEL // 2
HALF_DIM = 64
HEAD_DIM = 2 * HALF_DIM
N_HEADS = ATTN_WIDTH // HEAD_DIM
POOL_WIDTH = D_MODEL - ATTN_WIDTH
POOL_WINDOWS = (2, 4, 8, 16)
POOL_GROUP = POOL_WIDTH // len(POOL_WINDOWS)
POOL_BUF = max(POOL_WINDOWS) - 1
D_FF = ((8 * D_MODEL // 3 + 255) // 256) * 256
ROPE_THETA = 10000.0
N_ADA = 9
EPS = 1e-6
NEG = -1e30
LAM_INIT = 0.8 - 0.6 * math.exp(-0.3 * 0)

M_PROMPT = BATCH * SEQ
M_SAMPLE = DEC_BATCH * DEC_SEQ
M_ALL = M_PROMPT + M_SAMPLE

VMEM_LIMIT_BYTES = 58 * 1024 * 1024

ROW_TILE = 128
MM_ROWS = 768
STAGED_ROWS = 1056
MERGE_ROWS = 528
EPILOGUE_CHUNKS = 2
MERGE_CHUNKS = 3
LOG2_E = math.log2(math.e)
PAGES_PER_STEP = 8
HEADS_PER_GROUP = 8
Q_TILE = 512
Q_ROW_BLOCKS = 2


def _params(sem):
    return pltpu.CompilerParams(dimension_semantics=sem, vmem_limit_bytes=VMEM_LIMIT_BYTES)


def _sigmoid(x):
    return 1.0 / (1.0 + jnp.exp(-x))


def _rms(x, g):
    return x * lax.rsqrt(jnp.mean(x * x, axis=-1, keepdims=True) + EPS) * g


def _ada_kernel(c_ref, w_ref, b_ref, o_ref):
    c = c_ref[...]
    a = (c * _sigmoid(c)).astype(BF16)
    o_ref[...] = jnp.dot(a, w_ref[...].astype(BF16), preferred_element_type=F32) + b_ref[...]


def _ada(c_pad, w_ada, b_ada):
    rows = c_pad.shape[0]
    n = w_ada.shape[1]
    tn = 1024
    return pl.pallas_call(
        _ada_kernel,
        grid=(n // tn,),
        in_specs=[pl.BlockSpec((rows, D_MODEL), lambda j: (0, 0)),
                  pl.BlockSpec((D_MODEL, tn), lambda j: (0, j)),
                  pl.BlockSpec((1, tn), lambda j: (0, j))],
        out_specs=pl.BlockSpec((rows, tn), lambda j: (0, j)),
        out_shape=jax.ShapeDtypeStruct((rows, n), F32),
        compiler_params=_params(("arbitrary",)),
    )(c_pad, w_ada, b_ada.reshape(1, n))


_PROMPT_TILES = M_PROMPT // ROW_TILE
_TILES_PER_BATCH = SEQ // ROW_TILE


def _mod_group(i):
    return jnp.where(i < _PROMPT_TILES, i // _TILES_PER_BATCH, i - _PROMPT_TILES + BATCH)


def _mod_spec(j):
    return pl.BlockSpec((1, 1, ROW_TILE, D_MODEL), lambda i: (j, _mod_group(i), 0, 0))


def _row_spec():
    return pl.BlockSpec((ROW_TILE, D_MODEL), lambda i: (i, 0))


def _pair_specs():
    return [pl.BlockSpec((ROW_TILE, D_MODEL), lambda i: (jnp.minimum(i, _PROMPT_TILES - 1), 0)),
            pl.BlockSpec((ROW_TILE, D_MODEL), lambda i: (jnp.maximum(i - _PROMPT_TILES, 0), 0))]


def _pair_shapes():
    return [jax.ShapeDtypeStruct((M_PROMPT, D_MODEL), F32),
            jax.ShapeDtypeStruct((M_SAMPLE, D_MODEL), F32)]


def _vec_spec():
    return pl.BlockSpec((1, D_MODEL), lambda i: (0, 0))


def _read_pair(xp_ref, xs_ref):
    return jnp.where(pl.program_id(0) < _PROMPT_TILES, xp_ref[...], xs_ref[...])


def _write_pair(xp_ref, xs_ref, x):
    @pl.when(pl.program_id(0) < _PROMPT_TILES)
    def _():
        xp_ref[...] = x

    @pl.when(pl.program_id(0) >= _PROMPT_TILES)
    def _():
        xs_ref[...] = x


def _norm_mod_kernel(xp_ref, xs_ref, g_ref, scale_ref, shift_ref, h_ref):
    h = _rms(_read_pair(xp_ref, xs_ref), g_ref[...])
    h_ref[...] = (h * (1.0 + scale_ref[0, 0]) + shift_ref[0, 0]).astype(h_ref.dtype)


def _norm_mod(x, g, mod, j):
    return pl.pallas_call(
        _norm_mod_kernel,
        grid=(M_ALL // ROW_TILE,),
        in_specs=_pair_specs() + [_vec_spec(), _mod_spec(3 * j + 1), _mod_spec(3 * j)],
        out_specs=_row_spec(),
        out_shape=jax.ShapeDtypeStruct((M_ALL, D_MODEL), BF16),
        compiler_params=_params(("arbitrary",)),
    )(*x, g.reshape(1, D_MODEL), mod, mod)


def _resid_norm_mod_kernel(xp_ref, xs_ref, f_ref, gpost_ref, gate_ref, gpre_ref, scale_ref,
                           shift_ref, xpo_ref, xso_ref, h_ref, *, coef):
    x = _read_pair(xp_ref, xs_ref) + coef * gate_ref[0, 0] * _rms(f_ref[...], gpost_ref[...])
    _write_pair(xpo_ref, xso_ref, x)
    h = _rms(x, gpre_ref[...])
    h_ref[...] = (h * (1.0 + scale_ref[0, 0]) + shift_ref[0, 0]).astype(h_ref.dtype)


def _resid_norm_mod(x, f, gpost, gpre, mod, j_prev, j_next, coef):
    xp, xs, h = pl.pallas_call(
        functools.partial(_resid_norm_mod_kernel, coef=coef),
        grid=(M_ALL // ROW_TILE,),
        in_specs=_pair_specs() + [_row_spec(), _vec_spec(), _mod_spec(3 * j_prev + 2), _vec_spec(),
                                  _mod_spec(3 * j_next + 1), _mod_spec(3 * j_next)],
        out_specs=_pair_specs() + [_row_spec()],
        out_shape=_pair_shapes() + [jax.ShapeDtypeStruct((M_ALL, D_MODEL), BF16)],
        compiler_params=_params(("arbitrary",)),
    )(*x, f, gpost.reshape(1, D_MODEL), mod, gpre.reshape(1, D_MODEL), mod, mod)
    return (xp, xs), h


def _resid_kernel(xp_ref, xs_ref, f_ref, gpost_ref, gate_ref, xpo_ref, xso_ref, *, coef):
    x = _read_pair(xp_ref, xs_ref) + coef * gate_ref[0, 0] * _rms(f_ref[...], gpost_ref[...])
    _write_pair(xpo_ref, xso_ref, x)


def _resid(x, f, gpost, mod, j_prev, coef):
    return pl.pallas_call(
        functools.partial(_resid_kernel, coef=coef),
        grid=(M_ALL // ROW_TILE,),
        in_specs=_pair_specs() + [_row_spec(), _vec_spec(), _mod_spec(3 * j_prev + 2)],
        out_specs=_pair_specs(),
        out_shape=_pair_shapes(),
        compiler_params=_params(("arbitrary",)),
    )(*x, f, gpost.reshape(1, D_MODEL), mod)


def _cast_kernel(w_ref, o_ref):
    o_ref[...] = w_ref[...].astype(o_ref.dtype)


def _first_tile(w, col0, tn):
    k = w.shape[0]
    joff = col0 // tn
    return pl.pallas_call(
        _cast_kernel,
        grid=(1,),
        in_specs=[pl.BlockSpec((k, tn), lambda i: (0, joff))],
        out_specs=pl.BlockSpec((k, tn), lambda i: (0, 0)),
        out_shape=jax.ShapeDtypeStruct((k, tn), BF16),
        compiler_params=_params(("arbitrary",)),
    )(w)


def _chunk_spec(w, col0, tn, nj, ni):
    k = w.shape[0]
    assert k % ni == 0 and (k // ni) % 8 == 0 and col0 % tn == 0
    joff = col0 // tn
    return pl.BlockSpec((k // ni, tn), lambda j, i: (i, jnp.minimum(j + 1, nj - 1) + joff))


def _stage_weights(chunk_refs, first_refs, slot_refs):
    j, i = pl.program_id(0), pl.program_id(1)

    @pl.when((j == 0) & (i == 0))
    def _():
        for first, slots in zip(first_refs, slot_refs):
            pltpu.sync_copy(first, slots.at[0])

    @pl.when(j + 1 < pl.num_programs(0))
    def _():
        nxt = (j + 1) % 2
        for chunk, slots in zip(chunk_refs, slot_refs):
            ck = chunk.shape[0]
            slots[nxt, pl.ds(pl.multiple_of(i * ck, ck), ck), :] = chunk[...].astype(BF16)

    return j % 2


def _slot_scratch(w, tn):
    return pltpu.VMEM((2, w.shape[0], tn), BF16)


_HBM = pl.BlockSpec(memory_space=pl.ANY)


def _swiglu_up_kernel(a_ref, w1c, w3c, w1f, w3f, o_ref, w1q, w3q):
    slot = _stage_weights((w1c, w3c), (w1f, w3f), (w1q, w3q))
    rows = a_ref.shape[0] // EPILOGUE_CHUNKS
    for c in range(EPILOGUE_CHUNKS):
        sl = pl.ds(c * rows, rows)
        a = a_ref[sl, :]
        u = jnp.dot(a, w1q[slot], preferred_element_type=F32)
        v = jnp.dot(a, w3q[slot], preferred_element_type=F32)
        o_ref[sl, :] = (u * _sigmoid(u) * v).astype(o_ref.dtype)


def _swiglu_cols(h, w1, w3, col0, n, tn):
    tm = STAGED_ROWS
    ni, nj = M_ALL // tm, n // tn
    assert n % tn == 0 and M_ALL % tm == 0
    return pl.pallas_call(
        _swiglu_up_kernel,
        grid=(nj, ni),
        in_specs=[pl.BlockSpec((tm, D_MODEL), lambda j, i: (i, 0)),
                  _chunk_spec(w1, col0, tn, nj, ni), _chunk_spec(w3, col0, tn, nj, ni),
                  _HBM, _HBM],
        out_specs=pl.BlockSpec((tm, tn), lambda j, i: (i, j)),
        out_shape=jax.ShapeDtypeStruct((M_ALL, n), BF16),
        scratch_shapes=[_slot_scratch(w1, tn), _slot_scratch(w3, tn)],
        compiler_params=_params(("arbitrary", "arbitrary")),
    )(h, w1, w3, _first_tile(w1, col0, tn), _first_tile(w3, col0, tn))


def _swiglu_up(h, w1, w3):
    wide = (D_FF // 512) * 512
    return (_swiglu_cols(h, w1, w3, 0, wide, 512),
            _swiglu_cols(h, w1, w3, wide, D_FF - wide, D_FF - wide))


def _down_kernel(a0_ref, a1_ref, at_ref, w0_ref, w1_ref, o_ref):
    k = pl.program_id(1)
    last = pl.num_programs(1) - 1

    def first(a_ref):
        return jnp.dot(a_ref[...], w0_ref[...].astype(BF16), preferred_element_type=F32)

    def second():
        return jnp.dot(a1_ref[...], w1_ref[...].astype(BF16), preferred_element_type=F32)

    @pl.when(k == 0)
    def _():
        o_ref[...] = first(a0_ref) + second()

    @pl.when((k != 0) & (k != last))
    def _():
        o_ref[...] += first(a0_ref) + second()

    @pl.when(k == last)
    def _():
        o_ref[...] += first(at_ref)


def _down(g, w2):
    g_wide, g_tail = g
    tk = g_tail.shape[1]
    tm = MM_ROWS
    nk = D_FF // tk
    nw = g_wide.shape[1] // tk
    assert nk % 2 == 1 and nw == nk - 1
    return pl.pallas_call(
        _down_kernel,
        grid=(M_ALL // tm, (nk + 1) // 2),
        in_specs=[pl.BlockSpec((tm, tk), lambda i, k: (i, jnp.minimum(2 * k, nw - 2))),
                  pl.BlockSpec((tm, tk), lambda i, k: (i, jnp.minimum(2 * k + 1, nw - 1))),
                  pl.BlockSpec((tm, tk), lambda i, k: (i, 0)),
                  pl.BlockSpec((tk, D_MODEL), lambda i, k: (2 * k, 0)),
                  pl.BlockSpec((tk, D_MODEL), lambda i, k: (jnp.minimum(2 * k + 1, nk - 1), 0))],
        out_specs=pl.BlockSpec((tm, D_MODEL), lambda i, k: (i, 0)),
        out_shape=jax.ShapeDtypeStruct((M_ALL, D_MODEL), F32),
        compiler_params=_params(("arbitrary", "arbitrary")),
    )(g_wide, g_wide, g_tail, w2, w2)


def _proj_kernel(a_ref, wc, wf, *rest, rope, out_scale):
    if rope:
        cos_ref, sin_ref, o_ref, wq = rest
    else:
        o_ref, wq = rest
    slot = _stage_weights((wc,), (wf,), (wq,))
    rows = a_ref.shape[0] // EPILOGUE_CHUNKS
    tn = o_ref.shape[1]
    for c in range(EPILOGUE_CHUNKS):
        sl = pl.ds(c * rows, rows)
        y = jnp.dot(a_ref[sl, :], wq[slot], preferred_element_type=F32)
        if rope:
            reps = tn // HEAD_DIM
            cos = jnp.tile(cos_ref[sl, :], (1, reps))
            sin = jnp.tile(sin_ref[sl, :], (1, reps))
            lane = lax.broadcasted_iota(jnp.int32, y.shape, 1)
            first = (lane % HALF_DIM) < (HALF_DIM // 2)
            partner = jnp.where(first, pltpu.roll(y, tn - HALF_DIM // 2, 1),
                                pltpu.roll(y, HALF_DIM // 2, 1))
            y = y * cos + partner * sin
        if out_scale != 1.0:
            y = y * out_scale
        o_ref[sl, :] = y.astype(o_ref.dtype)


def _proj(a, w, col0, n, out_dtype, rows, rope_tabs=None, out_scale=1.0):
    row0, m, tm, tn = rows
    k = a.shape[1]
    ni, nj = m // tm, n // tn
    ioff = row0 // tm
    assert row0 % tm == 0 and m % tm == 0 and n % tn == 0
    in_specs = [pl.BlockSpec((tm, k), lambda j, i: (i + ioff, 0)),
                _chunk_spec(w, col0, tn, nj, ni), _HBM]
    args = [a, w, _first_tile(w, col0, tn)]
    if rope_tabs is not None:
        in_specs += [pl.BlockSpec((tm, HEAD_DIM), lambda j, i: (i + ioff, 0))] * 2
        args += list(rope_tabs)
    return pl.pallas_call(
        functools.partial(_proj_kernel, rope=rope_tabs is not None, out_scale=out_scale),
        grid=(nj, ni),
        in_specs=in_specs,
        out_specs=pl.BlockSpec((tm, tn), lambda j, i: (i, j)),
        out_shape=jax.ShapeDtypeStruct((m, n), out_dtype),
        scratch_shapes=[_slot_scratch(w, tn)],
        compiler_params=_params(("arbitrary", "arbitrary")),
    )(*args)


_ALL_ROWS = (0, M_ALL, STAGED_ROWS, 1024)
_PROMPT_ROWS = (0, M_PROMPT, 1024, 1024)
_SAMPLE_ROWS = (M_PROMPT, M_SAMPLE, M_SAMPLE, 512)


def _merge_kernel(h_ref, at_ref, po_ref, wg0c, wg1c, wac, wpc, wg0f, wg1f, waf, wpf, o_ref,
                  wg0q, wg1q, waq, wpq):
    slot = _stage_weights((wg0c, wg1c, wac, wpc), (wg0f, wg1f, waf, wpf), (wg0q, wg1q, waq, wpq))
    rows = h_ref.shape[0] // MERGE_CHUNKS
    for c in range(MERGE_CHUNKS):
        sl = pl.ds(c * rows, rows)
        h = h_ref[sl, :]
        g0 = _sigmoid(jnp.dot(h, wg0q[slot], preferred_element_type=F32))
        g1 = _sigmoid(jnp.dot(h, wg1q[slot], preferred_element_type=F32))
        a_up = jnp.dot(at_ref[sl, :], waq[slot], preferred_element_type=F32)
        p_up = jnp.dot(po_ref[sl, :], wpq[slot], preferred_element_type=F32)
        o_ref[sl, :] = (g0 * a_up + g1 * p_up).astype(o_ref.dtype)


def _merge(h, attn, pooled, w_gate, w_up_attn, w_up_pool):
    tn = 512
    tm = MERGE_ROWS
    ni, nj = M_ALL // tm, D_MODEL // tn
    cols = [(w_gate, 0), (w_gate, D_MODEL), (w_up_attn, 0), (w_up_pool, 0)]
    return pl.pallas_call(
        _merge_kernel,
        grid=(nj, ni),
        in_specs=[pl.BlockSpec((tm, D_MODEL), lambda j, i: (i, 0)),
                  pl.BlockSpec((tm, ATTN_WIDTH), lambda j, i: (i, 0)),
                  pl.BlockSpec((tm, POOL_WIDTH), lambda j, i: (i, 0))]
                 + [_chunk_spec(w, c0, tn, nj, ni) for w, c0 in cols] + [_HBM] * len(cols),
        out_specs=pl.BlockSpec((tm, tn), lambda j, i: (i, j)),
        out_shape=jax.ShapeDtypeStruct((M_ALL, D_MODEL), BF16),
        scratch_shapes=[_slot_scratch(w, tn) for w, _ in cols],
        compiler_params=_params(("arbitrary", "arbitrary")),
    )(h, attn, pooled, *[w for w, _ in cols], *[_first_tile(w, c0, tn) for w, c0 in cols])


def _lambda(lq1, lk1, lq2, lk2):
    return (jnp.exp(jnp.sum(lq1[...] * lk1[...], axis=-1, keepdims=True))
            - jnp.exp(jnp.sum(lq2[...] * lk2[...], axis=-1, keepdims=True)) + LAM_INIT)


def _split_halves(q):
    lane = lax.broadcasted_iota(jnp.int32, q.shape, 1)
    zero = jnp.zeros_like(q)
    return jnp.concatenate([jnp.where(lane < HALF_DIM, q, zero),
                            jnp.where(lane >= HALF_DIM, q, zero)], axis=0)


def _head_out(o, lam, g):
    t = o.shape[0] // 2
    a = o[:t] - lam * o[t:]
    return _rms(a, g) * (1.0 - LAM_INIT)


def _online_update(m_ref, acc_ref, idx, s, v_ones):
    m_prev = m_ref[idx]
    m_new = jnp.maximum(m_prev, jnp.max(s, axis=-1, keepdims=True))
    p = jnp.exp2(s - jnp.tile(m_new, (1, s.shape[1] // HEAD_DIM)))
    alpha = jnp.exp2(m_prev - m_new)
    acc_ref[idx] = (acc_ref[idx] * jnp.tile(alpha, (1, 2))
                    + jnp.dot(p.astype(BF16), v_ones, preferred_element_type=F32))
    m_ref[idx] = m_new


def _prompt_attn_kernel(q_ref, k_ref, v_ref, lq1, lk1, lq2, lk2, g_ref, init_ref, o_ref,
                        kb, vb, m_s, acc_s, s_s):
    del init_ref
    tq = Q_TILE
    kb[...] = k_ref[...].astype(BF16)
    vb[:, :HEAD_DIM] = v_ref[...].astype(BF16)
    vb[:, HEAD_DIM:] = jnp.ones((SEQ, HEAD_DIM), BF16)
    lam = _lambda(lq1, lk1, lq2, lk2)
    g = g_ref[...]

    n_chains, rows = m_s.shape[0], m_s.shape[1]
    row_blocks = n_chains // 2

    def q_tile(qi, carry):
        q0 = pl.multiple_of(qi * tq, tq)
        chains = []
        for c in range(n_chains):
            r0 = (c % row_blocks) * rows
            q = q_ref[pl.ds(q0 + r0, rows), :]
            lane = lax.broadcasted_iota(jnp.int32, q.shape, 1)
            keep = (lane < HALF_DIM) if c < row_blocks else (lane >= HALF_DIM)
            chains.append((r0, jnp.where(keep, q, jnp.zeros_like(q))))
        m_s[...] = jnp.full(m_s.shape, NEG, F32)
        acc_s[...] = jnp.zeros(acc_s.shape, F32)

        def scores(j, slot):
            k = kb[pl.ds(pl.multiple_of(j * tq, tq), tq), :]
            for c, (_, qc) in enumerate(chains):
                s_s[slot, c] = lax.dot_general(qc, k, (((1,), (1,)), ((), ())),
                                               preferred_element_type=F32)

        scores(0, 0)

        def body(j, inner):
            for slot in range(2):
                @pl.when(j % 2 == slot)
                def _():
                    scores(j + 1, 1 - slot)
                    v = vb[pl.ds(pl.multiple_of(j * tq, tq), tq), :]
                    for c in range(n_chains):
                        _online_update(m_s, acc_s, (c,), s_s[slot, c], v)
            return inner

        lax.fori_loop(0, qi, body, 0)

        v = vb[pl.ds(q0, tq), :]
        for c, (r0, _) in enumerate(chains):
            nk = r0 + rows
            s = s_s[qi % 2, c][:, :nk]
            row = r0 + lax.broadcasted_iota(jnp.int32, s.shape, 0)
            col = lax.broadcasted_iota(jnp.int32, s.shape, 1)
            _online_update(m_s, acc_s, (c,), jnp.where(col <= row, s, NEG), v[:nk])

        for rb in range(row_blocks):
            acc = jnp.concatenate([acc_s[rb], acc_s[row_blocks + rb]], axis=0)
            o = acc[:, :HEAD_DIM] / acc[:, HEAD_DIM:]
            o_ref[pl.ds(q0 + rb * rows, rows), :] = _head_out(o, lam, g).astype(o_ref.dtype)
        return carry

    lax.fori_loop(0, SEQ // tq, q_tile, 0)


def _prompt_attn(q, k, v, lams, subln_g):
    tq = Q_TILE
    small = pl.BlockSpec((1, HALF_DIM), lambda b, h: (0, 0))
    head = pl.BlockSpec((SEQ, HEAD_DIM), lambda b, h: (b, h))
    return pl.pallas_call(
        _prompt_attn_kernel,
        grid=(BATCH, N_HEADS),
        in_specs=[head, head, head, small, small, small, small,
                  pl.BlockSpec((1, HEAD_DIM), lambda b, h: (0, 0)), _HBM],
        out_specs=head,
        out_shape=jax.ShapeDtypeStruct((M_ALL, ATTN_WIDTH), BF16),
        input_output_aliases={8: 0},
        scratch_shapes=[pltpu.VMEM((SEQ, HEAD_DIM), BF16), pltpu.VMEM((SEQ, 2 * HEAD_DIM), BF16),
                        pltpu.VMEM((2 * Q_ROW_BLOCKS, tq // Q_ROW_BLOCKS, HEAD_DIM), F32),
                        pltpu.VMEM((2 * Q_ROW_BLOCKS, tq // Q_ROW_BLOCKS, 2 * HEAD_DIM), F32),
                        pltpu.VMEM((2, 2 * Q_ROW_BLOCKS, tq // Q_ROW_BLOCKS, tq), F32)],
        compiler_params=_params(("arbitrary", "arbitrary")),
    )(q, k, v, *lams, subln_g.reshape(1, HEAD_DIM), jnp.zeros((M_ALL, ATTN_WIDTH), BF16))


_GROUP_ROWS = HEADS_PER_GROUP * 2 * DEC_SEQ
_SLAB_ROWS = PAGE_SIZE * HEADS_PER_GROUP
_N_GROUPS = N_HEADS // HEADS_PER_GROUP


def _sample_attn_kernel(pt_ref, q_ref, kn_ref, vn_ref, *rest):
    del pt_ref
    kpages = rest[:PAGES_PER_STEP]
    vpages = rest[PAGES_PER_STEP:2 * PAGES_PER_STEP]
    (lq1, lk1, lq2, lk2, g_ref, attn_in, o_ref,
     qz_s, bias_s, m_s, acc_s, out_s) = rest[2 * PAGES_PER_STEP:]
    del attn_in
    b = pl.program_id(0)
    step = pl.program_id(1)
    last_step = pl.num_programs(1) - 1

    @pl.when(step == 0)
    def _():
        for h in range(N_HEADS):
            grp, hl = divmod(h, HEADS_PER_GROUP)
            qh = q_ref[:, h * HEAD_DIM:(h + 1) * HEAD_DIM]
            qz_s[grp, hl * 2 * DEC_SEQ:(hl + 1) * 2 * DEC_SEQ, :] = _split_halves(qh)
        m_s[...] = jnp.full(m_s.shape, NEG, F32)
        acc_s[...] = jnp.zeros(acc_s.shape, F32)

    @pl.when((b == 0) & (step == 0))
    def _():
        row = lax.broadcasted_iota(jnp.int32, bias_s.shape, 0)
        col = lax.broadcasted_iota(jnp.int32, bias_s.shape, 1)
        bias_s[...] = jnp.where(col % HEADS_PER_GROUP == row // (2 * DEC_SEQ), 0.0, NEG)

    ones = jnp.ones((PAGES_PER_STEP * _SLAB_ROWS, HEAD_DIM), BF16)
    for grp in range(_N_GROUPS):
        lo = grp * HEADS_PER_GROUP
        qz = qz_s[grp].astype(BF16)
        k2 = jnp.concatenate(
            [kp[0, :, lo:lo + HEADS_PER_GROUP, :].reshape(_SLAB_ROWS, HEAD_DIM).astype(BF16)
             for kp in kpages], axis=0)
        v2 = jnp.concatenate(
            [vp[0, :, lo:lo + HEADS_PER_GROUP, :].reshape(_SLAB_ROWS, HEAD_DIM).astype(BF16)
             for vp in vpages], axis=0)
        s = lax.dot_general(qz, k2, (((1,), (1,)), ((), ())), preferred_element_type=F32)
        s = s + bias_s[...]
        _online_update(m_s, acc_s, (grp,), s, jnp.concatenate([v2, ones], axis=1))

    @pl.when(step == last_step)
    def _():
        lam = _lambda(lq1, lk1, lq2, lk2)
        g = g_ref[...]
        out_rows = pl.ds(pl.multiple_of(b * DEC_SEQ, DEC_SEQ), DEC_SEQ)
        for h in range(N_HEADS):
            grp, hl = divmod(h, HEADS_PER_GROUP)
            rows = pl.ds(hl * 2 * DEC_SEQ, 2 * DEC_SEQ)
            cols = slice(h * HEAD_DIM, (h + 1) * HEAD_DIM)
            qh = qz_s[grp, rows, :]
            s = lax.dot_general(qh, kn_ref[:, cols], (((1,), (1,)), ((), ())),
                                preferred_element_type=F32)
            row = lax.broadcasted_iota(jnp.int32, s.shape, 0)
            col = lax.broadcasted_iota(jnp.int32, s.shape, 1)
            causal = col <= row % DEC_SEQ
            s = jnp.where(causal, s, NEG)
            m_prev = m_s[grp, rows, :]
            m_new = jnp.maximum(m_prev, jnp.max(s, axis=-1, keepdims=True))
            p = jnp.where(causal, jnp.exp2(s - m_new[:, :DEC_SEQ]), 0.0)
            alpha = jnp.exp2(m_prev - m_new)
            acc = acc_s[grp, rows, :]
            num = acc[:, :HEAD_DIM] * alpha + jnp.dot(p, vn_ref[:, cols],
                                                      preferred_element_type=F32)
            den = acc[:, HEAD_DIM:] * alpha + jnp.sum(p, axis=-1, keepdims=True)
            out_s[out_rows, cols] = _head_out(num / den, lam, g)

    @pl.when((b == pl.num_programs(0) - 1) & (step == last_step))
    def _():
        o_ref[...] = out_s[...].astype(o_ref.dtype)


def _sample_attn(q, k, v, cache_k, cache_v, page_table, lams, subln_g, attn_all):
    pps = PAGES_PER_STEP

    def new_spec():
        return pl.BlockSpec((DEC_SEQ, ATTN_WIDTH), lambda b, s, pt: (b, 0))

    def page_spec(p):
        return pl.BlockSpec((1, PAGE_SIZE, N_HEADS, HEAD_DIM),
                            lambda b, s, pt: (pt[b, s * pps + p], 0, 0, 0))

    small = pl.BlockSpec((1, HALF_DIM), lambda b, s, pt: (0, 0))
    in_specs = ([new_spec(), new_spec(), new_spec()]
                + [page_spec(p) for p in range(pps)] + [page_spec(p) for p in range(pps)]
                + [small, small, small, small,
                   pl.BlockSpec((1, HEAD_DIM), lambda b, s, pt: (0, 0)),
                   pl.BlockSpec(memory_space=pl.ANY)])
    grid_spec = pltpu.PrefetchScalarGridSpec(
        num_scalar_prefetch=1,
        grid=(DEC_BATCH, N_PAGES // pps),
        in_specs=in_specs,
        out_specs=pl.BlockSpec((M_SAMPLE, ATTN_WIDTH),
                               lambda b, s, pt: (M_PROMPT // M_SAMPLE, 0)),
        scratch_shapes=[pltpu.VMEM((_N_GROUPS, _GROUP_ROWS, HEAD_DIM), F32),
                        pltpu.VMEM((_GROUP_ROWS, pps * _SLAB_ROWS), F32),
                        pltpu.VMEM((_N_GROUPS, _GROUP_ROWS, HEAD_DIM), F32),
                        pltpu.VMEM((_N_GROUPS, _GROUP_ROWS, 2 * HEAD_DIM), F32),
                        pltpu.VMEM((M_SAMPLE, ATTN_WIDTH), F32)],
    )
    return pl.pallas_call(
        _sample_attn_kernel,
        grid_spec=grid_spec,
        out_shape=jax.ShapeDtypeStruct((M_ALL, ATTN_WIDTH), BF16),
        input_output_aliases={len(in_specs): 0},
        compiler_params=_params(("arbitrary", "arbitrary")),
    )(page_table, q, k, v, *([cache_k] * pps), *([cache_v] * pps), *lams,
      subln_g.reshape(1, HEAD_DIM), attn_all)


_HALO = 16


def _pool_groups(ext_ref, lead, t, pos, write):
    for g, w in enumerate(POOL_WINDOWS):
        cols = slice(g * POOL_GROUP, (g + 1) * POOL_GROUP)
        tot = None
        for s in range(w):
            part = ext_ref[lead + (pl.ds(_HALO - s, t), cols)]
            tot = part if tot is None else tot + part
        cnt = jnp.minimum(pos + 1, w).astype(F32)
        pooled = tot / cnt - ext_ref[lead + (pl.ds(_HALO, t), cols)]
        write(g, cols, pooled)


def _pool_prompt_kernel(u_ref, w_ref, scale_ref, init_ref, o_ref, ext_s, wq_s):
    del init_ref
    i = pl.program_id(1)
    t = u_ref.shape[0]

    @pl.when((pl.program_id(0) == 0) & (i == 0))
    def _():
        wq_s[...] = w_ref[...].astype(BF16)

    @pl.when(i == 0)
    def _():
        ext_s[0:_HALO, :] = jnp.zeros((_HALO, POOL_WIDTH), F32)

    @pl.when(i != 0)
    def _():
        ext_s[0:_HALO, :] = ext_s[t:t + _HALO, :]

    ext_s[_HALO:_HALO + t, :] = u_ref[...]
    pos = i * t + lax.broadcasted_iota(jnp.int32, (t, 1), 0)

    def write(g, cols, pooled):
        y = jnp.dot(pooled.astype(BF16), wq_s[g], preferred_element_type=F32)
        o_ref[:, cols] = (y * scale_ref[:, cols]).astype(o_ref.dtype)

    _pool_groups(ext_s, (), t, pos, write)


def _pool_prompt(u, w_pool_mix, pool_scale):
    t = 256
    nt = SEQ // t
    return pl.pallas_call(
        _pool_prompt_kernel,
        grid=(BATCH, nt),
        in_specs=[pl.BlockSpec((t, POOL_WIDTH), lambda b, i: (b * nt + i, 0)),
                  pl.BlockSpec((len(POOL_WINDOWS), POOL_GROUP, POOL_GROUP), lambda b, i: (0, 0, 0)),
                  pl.BlockSpec((1, POOL_WIDTH), lambda b, i: (0, 0)), _HBM],
        out_specs=pl.BlockSpec((t, POOL_WIDTH), lambda b, i: (b * nt + i, 0)),
        out_shape=jax.ShapeDtypeStruct((M_ALL, POOL_WIDTH), BF16),
        input_output_aliases={3: 0},
        scratch_shapes=[pltpu.VMEM((_HALO + t, POOL_WIDTH), F32),
                        pltpu.VMEM((len(POOL_WINDOWS), POOL_GROUP, POOL_GROUP), BF16)],
        compiler_params=_params(("arbitrary", "arbitrary")),
    )(u, w_pool_mix, pool_scale.reshape(1, POOL_WIDTH), jnp.zeros((M_ALL, POOL_WIDTH), BF16))


def _pool_sample_kernel(state_ref, u_ref, w_ref, scale_ref, pooled_in, o_ref, ext_s):
    del pooled_in
    ext_s[:, 0:_HALO, :] = state_ref[...]
    ext_s[:, _HALO:_HALO + DEC_SEQ, :] = u_ref[...]
    pos = PAST_LEN + lax.broadcasted_iota(jnp.int32, (1, DEC_SEQ, 1), 1)

    def write(g, cols, pooled):
        x = pooled.reshape(M_SAMPLE, POOL_GROUP).astype(BF16)
        y = jnp.dot(x, w_ref[g].astype(BF16), preferred_element_type=F32)
        o_ref[:, cols] = (y * scale_ref[:, cols]).astype(o_ref.dtype)

    _pool_groups(ext_s, (slice(None),), DEC_SEQ, pos, write)


def _pool_sample(state16, u3, w_pool_mix, pool_scale, pooled_all):
    ng = len(POOL_WINDOWS)
    return pl.pallas_call(
        _pool_sample_kernel,
        grid=(1,),
        in_specs=[pl.BlockSpec((DEC_BATCH, _HALO, POOL_WIDTH), lambda i: (0, 0, 0)),
                  pl.BlockSpec((DEC_BATCH, DEC_SEQ, POOL_WIDTH), lambda i: (0, 0, 0)),
                  pl.BlockSpec((ng, POOL_GROUP, POOL_GROUP), lambda i: (0, 0, 0)),
                  pl.BlockSpec((1, POOL_WIDTH), lambda i: (0, 0)),
                  pl.BlockSpec(memory_space=pl.ANY)],
        out_specs=pl.BlockSpec((M_SAMPLE, POOL_WIDTH), lambda i: (M_PROMPT // M_SAMPLE, 0)),
        out_shape=jax.ShapeDtypeStruct((M_ALL, POOL_WIDTH), BF16),
        input_output_aliases={4: 0},
        scratch_shapes=[pltpu.VMEM((DEC_BATCH, _HALO + DEC_SEQ, POOL_WIDTH), F32)],
        compiler_params=_params(("arbitrary",)),
    )(state16, u3, w_pool_mix, pool_scale.reshape(1, POOL_WIDTH), pooled_all)


def _rope_tables():
    inv = 1.0 / (ROPE_THETA ** (jnp.arange(0, HALF_DIM, 2, dtype=F32) / HALF_DIM))
    pos = jnp.concatenate([jnp.tile(jnp.arange(SEQ), BATCH),
                           jnp.tile(PAST_LEN + jnp.arange(DEC_SEQ), DEC_BATCH)])
    ang = pos.astype(F32)[:, None] * inv[None, :]
    cos, sin = jnp.cos(ang), jnp.sin(ang)
    cos = jnp.tile(cos, (1, HEAD_DIM // (HALF_DIM // 2)))
    sin = jnp.tile(jnp.concatenate([-sin, sin], axis=1), (1, 2))
    return cos, sin


def _mod_rows(ada):
    ada = ada[:BATCH + DEC_BATCH].reshape(BATCH + DEC_BATCH, N_ADA, D_MODEL)
    prompt = jnp.broadcast_to(ada[:BATCH].transpose(1, 0, 2)[:, :, None, :],
                              (N_ADA, BATCH, ROW_TILE, D_MODEL))
    sample = jnp.repeat(ada[BATCH:], DEC_SEQ, axis=0)
    sample = sample.reshape(M_SAMPLE // ROW_TILE, ROW_TILE, N_ADA, D_MODEL).transpose(2, 0, 1, 3)
    return jnp.concatenate([prompt, sample], axis=1)


def kernel(x_prompt, x_sample, cache_k, cache_v, state_pool, page_table, c_prompt, c_sample,
           w_ada, b_ada, g_pre_ffn1, w1_ffn1, w3_ffn1, w2_ffn1, g_post_ffn1,
           g_pre_mix, w_in, lambda_q1, lambda_k1, lambda_q2, lambda_k2, subln_g,
           w_pool_mix, pool_scale, w_up_attn, w_up_pool, w_merge_gate, w_out, g_post_mix,
           g_pre_ffn2, w1_ffn2, w3_ffn2, w2_ffn2, g_post_ffn2):
    x = (x_prompt.reshape(M_PROMPT, D_MODEL), x_sample.reshape(M_SAMPLE, D_MODEL))
    c = jnp.concatenate([c_prompt, c_sample])
    c = jnp.pad(c, ((0, 48 - c.shape[0]), (0, 0)))
    mod = _mod_rows(_ada(c, w_ada, b_ada))
    lams = [v.reshape(1, HALF_DIM) for v in (lambda_q1, lambda_k1, lambda_q2, lambda_k2)]

    h = _norm_mod(x, g_pre_ffn1, mod, 0)
    f = _down(_swiglu_up(h, w1_ffn1, w3_ffn1), w2_ffn1)
    x, h = _resid_norm_mod(x, f, g_post_ffn1, g_pre_mix, mod, 0, 1, 0.5)

    tabs = _rope_tables()
    q_scale = HALF_DIM ** -0.5 * LOG2_E
    q_p = _proj(h, w_in, 0, ATTN_WIDTH, BF16, _PROMPT_ROWS, tabs, q_scale)
    k_p = _proj(h, w_in, ATTN_WIDTH, ATTN_WIDTH, F32, _PROMPT_ROWS, tabs)
    v_p = _proj(h, w_in, 2 * ATTN_WIDTH, ATTN_WIDTH, F32, _PROMPT_ROWS)
    u_p = _proj(h, w_in, 3 * ATTN_WIDTH, POOL_WIDTH, F32, _PROMPT_ROWS)
    q_s = _proj(h, w_in, 0, ATTN_WIDTH, F32, _SAMPLE_ROWS, tabs, q_scale)
    k_s = _proj(h, w_in, ATTN_WIDTH, ATTN_WIDTH, F32, _SAMPLE_ROWS, tabs)
    v_s = _proj(h, w_in, 2 * ATTN_WIDTH, ATTN_WIDTH, F32, _SAMPLE_ROWS)
    u_s = _proj(h, w_in, 3 * ATTN_WIDTH, POOL_WIDTH, F32, _SAMPLE_ROWS)

    attn = _prompt_attn(q_p, k_p, v_p, lams, subln_g)
    attn = _sample_attn(q_s, k_s, v_s, cache_k, cache_v, page_table, lams, subln_g, attn)

    u_sample = u_s.reshape(DEC_BATCH, DEC_SEQ, POOL_WIDTH)
    state16 = jnp.pad(state_pool, ((0, 0), (_HALO - POOL_BUF, 0), (0, 0)))
    pooled = _pool_prompt(u_p, w_pool_mix, pool_scale)
    pooled = _pool_sample(state16, u_sample, w_pool_mix, pool_scale, pooled)

    merged = _merge(h, attn, pooled, w_merge_gate, w_up_attn, w_up_pool)
    m = _proj(merged, w_out, 0, D_MODEL, F32, _ALL_ROWS)
    x, h = _resid_norm_mod(x, m, g_post_mix, g_pre_ffn2, mod, 1, 2, 1.0)

    f = _down(_swiglu_up(h, w1_ffn2, w3_ffn2), w2_ffn2)
    y_prompt, y_sample = _resid(x, f, g_post_ffn2, mod, 2, 0.5)
    y_prompt = y_prompt.reshape(BATCH, SEQ, D_MODEL)
    y_sample = y_sample.reshape(DEC_BATCH, DEC_SEQ, D_MODEL)
    k_prompt = k_p.reshape(BATCH, SEQ, N_HEADS, HEAD_DIM)
    v_prompt = v_p.reshape(BATCH, SEQ, N_HEADS, HEAD_DIM)
    k_sample = k_s.reshape(DEC_BATCH, DEC_SEQ, N_HEADS, HEAD_DIM)
    v_sample = v_s.reshape(DEC_BATCH, DEC_SEQ, N_HEADS, HEAD_DIM)
    pool_prompt = u_p.reshape(BATCH, SEQ, POOL_WIDTH)[:, SEQ - POOL_BUF:]
    pool_sample = jnp.concatenate([state_pool, u_sample], axis=1)[:, -POOL_BUF:]
    return (y_prompt, y_sample, k_prompt, v_prompt, pool_prompt, k_sample, v_sample, pool_sample)
```

```python
import functools
import math

import jax
import jax.numpy as jnp
from jax import lax
from jax.experimental import pallas as pl
from jax.experimental.pallas import tpu as pltpu

F32 = jnp.float32
BF16 = jnp.bfloat16

D_MODEL = 4096
BATCH = 2
SEQ = 4096
DEC_BATCH = 32
DEC_SEQ = 8
PAST_LEN = 8192
PAGE_SIZE = 128
N_PAGES = PAST_LEN // PAGE_SIZE
ATTN_WIDTH = D_MODEL // 2
HALF_DIM = 64
HEAD_DIM = 2 * HALF_DIM
N_HEADS = ATTN_WIDTH // HEAD_DIM
POOL_WIDTH = D_MODEL - ATTN_WIDTH
POOL_WINDOWS = (2, 4, 8, 16)
POOL_GROUP = POOL_WIDTH // len(POOL_WINDOWS)
POOL_BUF = max(POOL_WINDOWS) - 1
D_FF = ((8 * D_MODEL // 3 + 255) // 256) * 256
ROPE_THETA = 10000.0
N_ADA = 9
EPS = 1e-6
NEG = -1e30
LAM_INIT = 0.8 - 0.6 * math.exp(-0.3 * 0)

M_PROMPT = BATCH * SEQ
M_SAMPLE = DEC_BATCH * DEC_SEQ
M_ALL = M_PROMPT + M_SAMPLE

VMEM_LIMIT_BYTES = 58 * 1024 * 1024

ROW_TILE = 128
MM_ROWS = 768
STAGED_ROWS = 1056
MERGE_ROWS = 528
EPILOGUE_CHUNKS = 2
MERGE_CHUNKS = 3
LOG2_E = math.log2(math.e)
PAGES_PER_STEP = 8
HEADS_PER_GROUP = 8
Q_TILE = 512
Q_ROW_BLOCKS = 2


def _params(sem):
    return pltpu.CompilerParams(dimension_semantics=sem, vmem_limit_bytes=VMEM_LIMIT_BYTES)


def _sigmoid(x):
    return 1.0 / (1.0 + jnp.exp(-x))


def _rms(x, g):
    return x * lax.rsqrt(jnp.mean(x * x, axis=-1, keepdims=True) + EPS) * g


def _ada_kernel(c_ref, w_ref, b_ref, o_ref):
    c = c_ref[...]
    a = (c * _sigmoid(c)).astype(BF16)
    o_ref[...] = jnp.dot(a, w_ref[...].astype(BF16), preferred_element_type=F32) + b_ref[...]


def _ada(c_pad, w_ada, b_ada):
    rows = c_pad.shape[0]
    n = w_ada.shape[1]
    tn = 1024
    return pl.pallas_call(
        _ada_kernel,
        grid=(n // tn,),
        in_specs=[pl.BlockSpec((rows, D_MODEL), lambda j: (0, 0)),
                  pl.BlockSpec((D_MODEL, tn), lambda j: (0, j)),
                  pl.BlockSpec((1, tn), lambda j: (0, j))],
        out_specs=pl.BlockSpec((rows, tn), lambda j: (0, j)),
        out_shape=jax.ShapeDtypeStruct((rows, n), F32),
        compiler_params=_params(("arbitrary",)),
    )(c_pad, w_ada, b_ada.reshape(1, n))


_PROMPT_TILES = M_PROMPT // ROW_TILE
_TILES_PER_BATCH = SEQ // ROW_TILE


def _mod_group(i):
    return jnp.where(i < _PROMPT_TILES, i // _TILES_PER_BATCH, i - _PROMPT_TILES + BATCH)


def _mod_spec(j):
    return pl.BlockSpec((1, 1, ROW_TILE, D_MODEL), lambda i: (j, _mod_group(i), 0, 0))


def _row_spec():
    return pl.BlockSpec((ROW_TILE, D_MODEL), lambda i: (i, 0))


def _pair_specs():
    return [pl.BlockSpec((ROW_TILE, D_MODEL), lambda i: (jnp.minimum(i, _PROMPT_TILES - 1), 0)),
            pl.BlockSpec((ROW_TILE, D_MODEL), lambda i: (jnp.maximum(i - _PROMPT_TILES, 0), 0))]


def _pair_shapes():
    return [jax.ShapeDtypeStruct((M_PROMPT, D_MODEL), F32),
            jax.ShapeDtypeStruct((M_SAMPLE, D_MODEL), F32)]


def _vec_spec():
    return pl.BlockSpec((1, D_MODEL), lambda i: (0, 0))


def _read_pair(xp_ref, xs_ref):
    return jnp.where(pl.program_id(0) < _PROMPT_TILES, xp_ref[...], xs_ref[...])


def _write_pair(xp_ref, xs_ref, x):
    @pl.when(pl.program_id(0) < _PROMPT_TILES)
    def _():
        xp_ref[...] = x

    @pl.when(pl.program_id(0) >= _PROMPT_TILES)
    def _():
        xs_ref[...] = x


def _norm_mod_kernel(xp_ref, xs_ref, g_ref, scale_ref, shift_ref, h_ref):
    h = _rms(_read_pair(xp_ref, xs_ref), g_ref[...])
    h_ref[...] = (h * (1.0 + scale_ref[0, 0]) + shift_ref[0, 0]).astype(h_ref.dtype)


def _norm_mod(x, g, mod, j):
    return pl.pallas_call(
        _norm_mod_kernel,
        grid=(M_ALL // ROW_TILE,),
        in_specs=_pair_specs() + [_vec_spec(), _mod_spec(3 * j + 1), _mod_spec(3 * j)],
        out_specs=_row_spec(),
        out_shape=jax.ShapeDtypeStruct((M_ALL, D_MODEL), BF16),
        compiler_params=_params(("arbitrary",)),
    )(*x, g.reshape(1, D_MODEL), mod, mod)


def _resid_norm_mod_kernel(xp_ref, xs_ref, f_ref, gpost_ref, gate_ref, gpre_ref, scale_ref,
                           shift_ref, xpo_ref, xso_ref, h_ref, *, coef):
    x = _read_pair(xp_ref, xs_ref) + coef * gate_ref[0, 0] * _rms(f_ref[...], gpost_ref[...])
    _write_pair(xpo_ref, xso_ref, x)
    h = _rms(x, gpre_ref[...])
    h_ref[...] = (h * (1.0 + scale_ref[0, 0]) + shift_ref[0, 0]).astype(h_ref.dtype)


def _resid_norm_mod(x, f, gpost, gpre, mod, j_prev, j_next, coef):
    xp, xs, h = pl.pallas_call(
        functools.partial(_resid_norm_mod_kernel, coef=coef),
        grid=(M_ALL // ROW_TILE,),
        in_specs=_pair_specs() + [_row_spec(), _vec_spec(), _mod_spec(3 * j_prev + 2), _vec_spec(),
                                  _mod_spec(3 * j_next + 1), _mod_spec(3 * j_next)],
        out_specs=_pair_specs() + [_row_spec()],
        out_shape=_pair_shapes() + [jax.ShapeDtypeStruct((M_ALL, D_MODEL), BF16)],
        compiler_params=_params(("arbitrary",)),
    )(*x, f, gpost.reshape(1, D_MODEL), mod, gpre.reshape(1, D_MODEL), mod, mod)
    return (xp, xs), h


def _resid_kernel(xp_ref, xs_ref, f_ref, gpost_ref, gate_ref, xpo_ref, xso_ref, *, coef):
    x = _read_pair(xp_ref, xs_ref) + coef * gate_ref[0, 0] * _rms(f_ref[...], gpost_ref[...])
    _write_pair(xpo_ref, xso_ref, x)


def _resid(x, f, gpost, mod, j_prev, coef):
    return pl.pallas_call(
        functools.partial(_resid_kernel, coef=coef),
        grid=(M_ALL // ROW_TILE,),
        in_specs=_pair_specs() + [_row_spec(), _vec_spec(), _mod_spec(3 * j_prev + 2)],
        out_specs=_pair_specs(),
        out_shape=_pair_shapes(),
        compiler_params=_params(("arbitrary",)),
    )(*x, f, gpost.reshape(1, D_MODEL), mod)


def _cast_kernel(w_ref, o_ref):
    o_ref[...] = w_ref[...].astype(o_ref.dtype)


def _first_tile(w, col0, tn):
    k = w.shape[0]
    joff = col0 // tn
    return pl.pallas_call(
        _cast_kernel,
        grid=(1,),
        in_specs=[pl.BlockSpec((k, tn), lambda i: (0, joff))],
        out_specs=pl.BlockSpec((k, tn), lambda i: (0, 0)),
        out_shape=jax.ShapeDtypeStruct((k, tn), BF16),
        compiler_params=_params(("arbitrary",)),
    )(w)


def _chunk_spec(w, col0, tn, nj, ni):
    k = w.shape[0]
    assert k % ni == 0 and (k // ni) % 8 == 0 and col0 % tn == 0
    joff = col0 // tn
    return pl.BlockSpec((k // ni, tn), lambda j, i: (i, jnp.minimum(j + 1, nj - 1) + joff))


def _stage_weights(chunk_refs, first_refs, slot_refs):
    j, i = pl.program_id(0), pl.program_id(1)

    @pl.when((j == 0) & (i == 0))
    def _():
        for first, slots in zip(first_refs, slot_refs):
            pltpu.sync_copy(first, slots.at[0])

    @pl.when(j + 1 < pl.num_programs(0))
    def _():
        nxt = (j + 1) % 2
        for chunk, slots in zip(chunk_refs, slot_refs):
            ck = chunk.shape[0]
            slots[nxt, pl.ds(pl.multiple_of(i * ck, ck), ck), :] = chunk[...].astype(BF16)

    return j % 2


def _slot_scratch(w, tn):
    return pltpu.VMEM((2, w.shape[0], tn), BF16)


_HBM = pl.BlockSpec(memory_space=pl.ANY)


def _swiglu_up_kernel(a_ref, w1c, w3c, w1f, w3f, w2_ref, o_ref, w2q_ref, w1q, w3q):
    slot = _stage_weights((w1c, w3c), (w1f, w3f), (w1q, w3q))
    w2q_ref[...] = w2_ref[...].astype(BF16)
    rows = a_ref.shape[0] // EPILOGUE_CHUNKS
    for c in range(EPILOGUE_CHUNKS):
        sl = pl.ds(c * rows, rows)
        a = a_ref[sl, :]
        u = jnp.dot(a, w1q[slot], preferred_element_type=F32)
        v = jnp.dot(a, w3q[slot], preferred_element_type=F32)
        o_ref[sl, :] = (u * _sigmoid(u) * v).astype(o_ref.dtype)


def _swiglu_cols(h, w1, w3, w2, col0, n, tn):
    tm = STAGED_ROWS
    ni, nj = M_ALL // tm, n // tn
    rb = n // (nj * ni)
    assert n % tn == 0 and M_ALL % tm == 0 and n % (nj * ni) == 0 and rb % 16 == 0
    roff = col0 // rb
    return pl.pallas_call(
        _swiglu_up_kernel,
        grid=(nj, ni),
        in_specs=[pl.BlockSpec((tm, D_MODEL), lambda j, i: (i, 0)),
                  _chunk_spec(w1, col0, tn, nj, ni), _chunk_spec(w3, col0, tn, nj, ni),
                  _HBM, _HBM,
                  pl.BlockSpec((rb, D_MODEL), lambda j, i: (roff + j * ni + i, 0))],
        out_specs=[pl.BlockSpec((tm, tn), lambda j, i: (i, j)),
                   pl.BlockSpec((rb, D_MODEL), lambda j, i: (j * ni + i, 0))],
        out_shape=[jax.ShapeDtypeStruct((M_ALL, n), BF16),
                   jax.ShapeDtypeStruct((n, D_MODEL), BF16)],
        scratch_shapes=[_slot_scratch(w1, tn), _slot_scratch(w3, tn)],
        compiler_params=_params(("arbitrary", "arbitrary")),
    )(h, w1, w3, _first_tile(w1, col0, tn), _first_tile(w3, col0, tn), w2)


def _swiglu_up(h, w1, w3, w2):
    wide = (D_FF // 512) * 512
    g_wide, w2_wide = _swiglu_cols(h, w1, w3, w2, 0, wide, 512)
    g_tail, w2_tail = _swiglu_cols(h, w1, w3, w2, wide, D_FF - wide, D_FF - wide)
    return (g_wide, g_tail), (w2_wide, w2_tail)


def _down_kernel(a0_ref, a1_ref, at_ref, w0_ref, w1_ref, wt_ref, o_ref):
    k = pl.program_id(1)
    last = pl.num_programs(1) - 1

    def pair():
        return (jnp.dot(a0_ref[...], w0_ref[...], preferred_element_type=F32)
                + jnp.dot(a1_ref[...], w1_ref[...], preferred_element_type=F32))

    @pl.when(k == 0)
    def _():
        o_ref[...] = pair()

    @pl.when((k != 0) & (k != last))
    def _():
        o_ref[...] += pair()

    @pl.when(k == last)
    def _():
        o_ref[...] += jnp.dot(at_ref[...], wt_ref[...], preferred_element_type=F32)


def _down(g, w2q):
    g_wide, g_tail = g
    w_wide, w_tail = w2q
    tk = g_tail.shape[1]
    tm = MM_ROWS
    nk = D_FF // tk
    nw = g_wide.shape[1] // tk
    assert nk % 2 == 1 and nw == nk - 1 and w_wide.shape[0] == nw * tk and w_tail.shape[0] == tk

    def even(k):
        return jnp.minimum(2 * k, nw - 2)

    def odd(k):
        return jnp.minimum(2 * k + 1, nw - 1)

    return pl.pallas_call(
        _down_kernel,
        grid=(M_ALL // tm, (nk + 1) // 2),
        in_specs=[pl.BlockSpec((tm, tk), lambda i, k: (i, even(k))),
                  pl.BlockSpec((tm, tk), lambda i, k: (i, odd(k))),
                  pl.BlockSpec((tm, tk), lambda i, k: (i, 0)),
                  pl.BlockSpec((tk, D_MODEL), lambda i, k: (even(k), 0)),
                  pl.BlockSpec((tk, D_MODEL), lambda i, k: (odd(k), 0)),
                  pl.BlockSpec((tk, D_MODEL), lambda i, k: (0, 0))],
        out_specs=pl.BlockSpec((tm, D_MODEL), lambda i, k: (i, 0)),
        out_shape=jax.ShapeDtypeStruct((M_ALL, D_MODEL), F32),
        compiler_params=_params(("arbitrary", "arbitrary")),
    )(g_wide, g_wide, g_tail, w_wide, w_wide, w_tail)


def _proj_kernel(a_ref, wc, wf, *rest, rope, out_scale):
    if rope:
        cos_ref, sin_ref, o_ref, wq = rest
    else:
        o_ref, wq = rest
    slot = _stage_weights((wc,), (wf,), (wq,))
    rows = a_ref.shape[0] // EPILOGUE_CHUNKS
    tn = o_ref.shape[1]
    for c in range(EPILOGUE_CHUNKS):
        sl = pl.ds(c * rows, rows)
        y = jnp.dot(a_ref[sl, :], wq[slot], preferred_element_type=F32)
        if rope:
            reps = tn // HEAD_DIM
            cos = jnp.tile(cos_ref[sl, :], (1, reps))
            sin = jnp.tile(sin_ref[sl, :], (1, reps))
            lane = lax.broadcasted_iota(jnp.int32, y.shape, 1)
            first = (lane % HALF_DIM) < (HALF_DIM // 2)
            partner = jnp.where(first, pltpu.roll(y, tn - HALF_DIM // 2, 1),
                                pltpu.roll(y, HALF_DIM // 2, 1))
            y = y * cos + partner * sin
        if out_scale != 1.0:
            y = y * out_scale
        o_ref[sl, :] = y.astype(o_ref.dtype)


def _proj(a, w, col0, n, out_dtype, rows, rope_tabs=None, out_scale=1.0):
    row0, m, tm, tn = rows
    k = a.shape[1]
    ni, nj = m // tm, n // tn
    ioff = row0 // tm
    assert row0 % tm == 0 and m % tm == 0 and n % tn == 0
    in_specs = [pl.BlockSpec((tm, k), lambda j, i: (i + ioff, 0)),
                _chunk_spec(w, col0, tn, nj, ni), _HBM]
    args = [a, w, _first_tile(w, col0, tn)]
    if rope_tabs is not None:
        in_specs += [pl.BlockSpec((tm, HEAD_DIM), lambda j, i: (i + ioff, 0))] * 2
        args += list(rope_tabs)
    return pl.pallas_call(
        functools.partial(_proj_kernel, rope=rope_tabs is not None, out_scale=out_scale),
        grid=(nj, ni),
        in_specs=in_specs,
        out_specs=pl.BlockSpec((tm, tn), lambda j, i: (i, j)),
        out_shape=jax.ShapeDtypeStruct((m, n), out_dtype),
        scratch_shapes=[_slot_scratch(w, tn)],
        compiler_params=_params(("arbitrary", "arbitrary")),
    )(*args)


_ALL_ROWS = (0, M_ALL, STAGED_ROWS, 1024)
_PROMPT_ROWS = (0, M_PROMPT, 1024, 1024)
_SAMPLE_ROWS = (M_PROMPT, M_SAMPLE, M_SAMPLE, 512)


def _merge_kernel(h_ref, at_ref, po_ref, wg0c, wg1c, wac, wpc, wg0f, wg1f, waf, wpf, o_ref,
                  wg0q, wg1q, waq, wpq):
    slot = _stage_weights((wg0c, wg1c, wac, wpc), (wg0f, wg1f, waf, wpf), (wg0q, wg1q, waq, wpq))
    rows = h_ref.shape[0] // MERGE_CHUNKS
    for c in range(MERGE_CHUNKS):
        sl = pl.ds(c * rows, rows)
        h = h_ref[sl, :]
        g0 = _sigmoid(jnp.dot(h, wg0q[slot], preferred_element_type=F32))
        g1 = _sigmoid(jnp.dot(h, wg1q[slot], preferred_element_type=F32))
        a_up = jnp.dot(at_ref[sl, :], waq[slot], preferred_element_type=F32)
        p_up = jnp.dot(po_ref[sl, :], wpq[slot], preferred_element_type=F32)
        o_ref[sl, :] = (g0 * a_up + g1 * p_up).astype(o_ref.dtype)


def _merge(h, attn, pooled, w_gate, w_up_attn, w_up_pool):
    tn = 512
    tm = MERGE_ROWS
    ni, nj = M_ALL // tm, D_MODEL // tn
    cols = [(w_gate, 0), (w_gate, D_MODEL), (w_up_attn, 0), (w_up_pool, 0)]
    return pl.pallas_call(
        _merge_kernel,
        grid=(nj, ni),
        in_specs=[pl.BlockSpec((tm, D_MODEL), lambda j, i: (i, 0)),
                  pl.BlockSpec((tm, ATTN_WIDTH), lambda j, i: (i, 0)),
                  pl.BlockSpec((tm, POOL_WIDTH), lambda j, i: (i, 0))]
                 + [_chunk_spec(w, c0, tn, nj, ni) for w, c0 in cols] + [_HBM] * len(cols),
        out_specs=pl.BlockSpec((tm, tn), lambda j, i: (i, j)),
        out_shape=jax.ShapeDtypeStruct((M_ALL, D_MODEL), BF16),
        scratch_shapes=[_slot_scratch(w, tn) for w, _ in cols],
        compiler_params=_params(("arbitrary", "arbitrary")),
    )(h, attn, pooled, *[w for w, _ in cols], *[_first_tile(w, c0, tn) for w, c0 in cols])


def _lambda(lq1, lk1, lq2, lk2):
    return (jnp.exp(jnp.sum(lq1[...] * lk1[...], axis=-1, keepdims=True))
            - jnp.exp(jnp.sum(lq2[...] * lk2[...], axis=-1, keepdims=True)) + LAM_INIT)


def _split_halves(q):
    lane = lax.broadcasted_iota(jnp.int32, q.shape, 1)
    zero = jnp.zeros_like(q)
    return jnp.concatenate([jnp.where(lane < HALF_DIM, q, zero),
                            jnp.where(lane >= HALF_DIM, q, zero)], axis=0)


def _head_out(o, lam, g):
    t = o.shape[0] // 2
    a = o[:t] - lam * o[t:]
    return _rms(a, g) * (1.0 - LAM_INIT)


def _online_update(m_ref, acc_ref, idx, s, v_ones):
    m_prev = m_ref[idx]
    m_new = jnp.maximum(m_prev, jnp.max(s, axis=-1, keepdims=True))
    p = jnp.exp2(s - jnp.tile(m_new, (1, s.shape[1] // HEAD_DIM)))
    alpha = jnp.exp2(m_prev - m_new)
    acc_ref[idx] = (acc_ref[idx] * jnp.tile(alpha, (1, 2))
                    + jnp.dot(p.astype(BF16), v_ones, preferred_element_type=F32))
    m_ref[idx] = m_new


def _prompt_attn_kernel(q_ref, k_ref, v_ref, lq1, lk1, lq2, lk2, g_ref, init_ref, o_ref,
                        kb, vb, m_s, acc_s, s_s):
    del init_ref
    tq = Q_TILE
    kb[...] = k_ref[...].astype(BF16)
    vb[:, :HEAD_DIM] = v_ref[...].astype(BF16)
    vb[:, HEAD_DIM:] = jnp.ones((SEQ, HEAD_DIM), BF16)
    lam = _lambda(lq1, lk1, lq2, lk2)
    g = g_ref[...]

    n_chains, rows = m_s.shape[0], m_s.shape[1]
    row_blocks = n_chains // 2

    def q_tile(qi, carry):
        q0 = pl.multiple_of(qi * tq, tq)
        chains = []
        for c in range(n_chains):
            r0 = (c % row_blocks) * rows
            q = q_ref[pl.ds(q0 + r0, rows), :]
            lane = lax.broadcasted_iota(jnp.int32, q.shape, 1)
            keep = (lane < HALF_DIM) if c < row_blocks else (lane >= HALF_DIM)
            chains.append((r0, jnp.where(keep, q, jnp.zeros_like(q))))
        m_s[...] = jnp.full(m_s.shape, NEG, F32)
        acc_s[...] = jnp.zeros(acc_s.shape, F32)

        def scores(j, slot):
            k = kb[pl.ds(pl.multiple_of(j * tq, tq), tq), :]
            for c, (_, qc) in enumerate(chains):
                s_s[slot, c] = lax.dot_general(qc, k, (((1,), (1,)), ((), ())),
                                               preferred_element_type=F32)

        scores(0, 0)

        def body(j, inner):
            for slot in range(2):
                @pl.when(j % 2 == slot)
                def _():
                    scores(j + 1, 1 - slot)
                    v = vb[pl.ds(pl.multiple_of(j * tq, tq), tq), :]
                    for c in range(n_chains):
                        _online_update(m_s, acc_s, (c,), s_s[slot, c], v)
            return inner

        lax.fori_loop(0, qi, body, 0)

        v = vb[pl.ds(q0, tq), :]
        for c, (r0, _) in enumerate(chains):
            nk = r0 + rows
            s = s_s[qi % 2, c][:, :nk]
            row = r0 + lax.broadcasted_iota(jnp.int32, s.shape, 0)
            col = lax.broadcasted_iota(jnp.int32, s.shape, 1)
            _online_update(m_s, acc_s, (c,), jnp.where(col <= row, s, NEG), v[:nk])

        for rb in range(row_blocks):
            acc = jnp.concatenate([acc_s[rb], acc_s[row_blocks + rb]], axis=0)
            o = acc[:, :HEAD_DIM] / acc[:, HEAD_DIM:]
            o_ref[pl.ds(q0 + rb * rows, rows), :] = _head_out(o, lam, g).astype(o_ref.dtype)
        return carry

    lax.fori_loop(0, SEQ // tq, q_tile, 0)


def _prompt_attn(q, k, v, lams, subln_g):
    tq = Q_TILE
    small = pl.BlockSpec((1, HALF_DIM), lambda b, h: (0, 0))
    head = pl.BlockSpec((SEQ, HEAD_DIM), lambda b, h: (b, h))
    return pl.pallas_call(
        _prompt_attn_kernel,
        grid=(BATCH, N_HEADS),
        in_specs=[head, head, head, small, small, small, small,
                  pl.BlockSpec((1, HEAD_DIM), lambda b, h: (0, 0)), _HBM],
        out_specs=head,
        out_shape=jax.ShapeDtypeStruct((M_ALL, ATTN_WIDTH), BF16),
        input_output_aliases={8: 0},
        scratch_shapes=[pltpu.VMEM((SEQ, HEAD_DIM), BF16), pltpu.VMEM((SEQ, 2 * HEAD_DIM), BF16),
                        pltpu.VMEM((2 * Q_ROW_BLOCKS, tq // Q_ROW_BLOCKS, HEAD_DIM), F32),
                        pltpu.VMEM((2 * Q_ROW_BLOCKS, tq // Q_ROW_BLOCKS, 2 * HEAD_DIM), F32),
                        pltpu.VMEM((2, 2 * Q_ROW_BLOCKS, tq // Q_ROW_BLOCKS, tq), F32)],
        compiler_params=_params(("arbitrary", "arbitrary")),
    )(q, k, v, *lams, subln_g.reshape(1, HEAD_DIM), jnp.zeros((M_ALL, ATTN_WIDTH), BF16))


_GROUP_ROWS = HEADS_PER_GROUP * 2 * DEC_SEQ
_SLAB_ROWS = PAGE_SIZE * HEADS_PER_GROUP
_N_GROUPS = N_HEADS // HEADS_PER_GROUP


def _sample_attn_kernel(pt_ref, q_ref, kn_ref, vn_ref, *rest):
    del pt_ref
    kpages = rest[:PAGES_PER_STEP]
    vpages = rest[PAGES_PER_STEP:2 * PAGES_PER_STEP]
    (lq1, lk1, lq2, lk2, g_ref, attn_in, o_ref,
     qz_s, bias_s, m_s, acc_s, out_s) = rest[2 * PAGES_PER_STEP:]
    del attn_in
    b = pl.program_id(0)
    step = pl.program_id(1)
    last_step = pl.num_programs(1) - 1

    @pl.when(step == 0)
    def _():
        for h in range(N_HEADS):
            grp, hl = divmod(h, HEADS_PER_GROUP)
            qh = q_ref[:, h * HEAD_DIM:(h + 1) * HEAD_DIM]
            qz_s[grp, hl * 2 * DEC_SEQ:(hl + 1) * 2 * DEC_SEQ, :] = _split_halves(qh)
        m_s[...] = jnp.full(m_s.shape, NEG, F32)
        acc_s[...] = jnp.zeros(acc_s.shape, F32)

    @pl.when((b == 0) & (step == 0))
    def _():
        row = lax.broadcasted_iota(jnp.int32, bias_s.shape, 0)
        col = lax.broadcasted_iota(jnp.int32, bias_s.shape, 1)
        bias_s[...] = jnp.where(col % HEADS_PER_GROUP == row // (2 * DEC_SEQ), 0.0, NEG)

    ones = jnp.ones((PAGES_PER_STEP * _SLAB_ROWS, HEAD_DIM), BF16)
    for grp in range(_N_GROUPS):
        lo = grp * HEADS_PER_GROUP
        qz = qz_s[grp].astype(BF16)
        k2 = jnp.concatenate(
            [kp[0, :, lo:lo + HEADS_PER_GROUP, :].reshape(_SLAB_ROWS, HEAD_DIM).astype(BF16)
             for kp in kpages], axis=0)
        v2 = jnp.concatenate(
            [vp[0, :, lo:lo + HEADS_PER_GROUP, :].reshape(_SLAB_ROWS, HEAD_DIM).astype(BF16)
             for vp in vpages], axis=0)
        s = lax.dot_general(qz, k2, (((1,), (1,)), ((), ())), preferred_element_type=F32)
        s = s + bias_s[...]
        _online_update(m_s, acc_s, (grp,), s, jnp.concatenate([v2, ones], axis=1))

    @pl.when(step == last_step)
    def _():
        lam = _lambda(lq1, lk1, lq2, lk2)
        g = g_ref[...]
        out_rows = pl.ds(pl.multiple_of(b * DEC_SEQ, DEC_SEQ), DEC_SEQ)
        for h in range(N_HEADS):
            grp, hl = divmod(h, HEADS_PER_GROUP)
            rows = pl.ds(hl * 2 * DEC_SEQ, 2 * DEC_SEQ)
            cols = slice(h * HEAD_DIM, (h + 1) * HEAD_DIM)
            qh = qz_s[grp, rows, :]
            s = lax.dot_general(qh, kn_ref[:, cols], (((1,), (1,)), ((), ())),
                                preferred_element_type=F32)
            row = lax.broadcasted_iota(jnp.int32, s.shape, 0)
            col = lax.broadcasted_iota(jnp.int32, s.shape, 1)
            causal = col <= row % DEC_SEQ
            s = jnp.where(causal, s, NEG)
            m_prev = m_s[grp, rows, :]
            m_new = jnp.maximum(m_prev, jnp.max(s, axis=-1, keepdims=True))
            p = jnp.where(causal, jnp.exp2(s - m_new[:, :DEC_SEQ]), 0.0)
            alpha = jnp.exp2(m_prev - m_new)
            acc = acc_s[grp, rows, :]
            num = acc[:, :HEAD_DIM] * alpha + jnp.dot(p, vn_ref[:, cols],
                                                      preferred_element_type=F32)
            den = acc[:, HEAD_DIM:] * alpha + jnp.sum(p, axis=-1, keepdims=True)
            out_s[out_rows, cols] = _head_out(num / den, lam, g)

    @pl.when((b == pl.num_programs(0) - 1) & (step == last_step))
    def _():
        o_ref[...] = out_s[...].astype(o_ref.dtype)


def _sample_attn(q, k, v, cache_k, cache_v, page_table, lams, subln_g, attn_all):
    pps = PAGES_PER_STEP

    def new_spec():
        return pl.BlockSpec((DEC_SEQ, ATTN_WIDTH), lambda b, s, pt: (b, 0))

    def page_spec(p):
        return pl.BlockSpec((1, PAGE_SIZE, N_HEADS, HEAD_DIM),
                            lambda b, s, pt: (pt[b, s * pps + p], 0, 0, 0))

    small = pl.BlockSpec((1, HALF_DIM), lambda b, s, pt: (0, 0))
    in_specs = ([new_spec(), new_spec(), new_spec()]
                + [page_spec(p) for p in range(pps)] + [page_spec(p) for p in range(pps)]
                + [small, small, small, small,
                   pl.BlockSpec((1, HEAD_DIM), lambda b, s, pt: (0, 0)),
                   pl.BlockSpec(memory_space=pl.ANY)])
    grid_spec = pltpu.PrefetchScalarGridSpec(
        num_scalar_prefetch=1,
        grid=(DEC_BATCH, N_PAGES // pps),
        in_specs=in_specs,
        out_specs=pl.BlockSpec((M_SAMPLE, ATTN_WIDTH),
                               lambda b, s, pt: (M_PROMPT // M_SAMPLE, 0)),
        scratch_shapes=[pltpu.VMEM((_N_GROUPS, _GROUP_ROWS, HEAD_DIM), F32),
                        pltpu.VMEM((_GROUP_ROWS, pps * _SLAB_ROWS), F32),
                        pltpu.VMEM((_N_GROUPS, _GROUP_ROWS, HEAD_DIM), F32),
                        pltpu.VMEM((_N_GROUPS, _GROUP_ROWS, 2 * HEAD_DIM), F32),
                        pltpu.VMEM((M_SAMPLE, ATTN_WIDTH), F32)],
    )
    return pl.pallas_call(
        _sample_attn_kernel,
        grid_spec=grid_spec,
        out_shape=jax.ShapeDtypeStruct((M_ALL, ATTN_WIDTH), BF16),
        input_output_aliases={len(in_specs): 0},
        compiler_params=_params(("arbitrary", "arbitrary")),
    )(page_table, q, k, v, *([cache_k] * pps), *([cache_v] * pps), *lams,
      subln_g.reshape(1, HEAD_DIM), attn_all)


_HALO = 16


def _pool_groups(ext_ref, lead, t, pos, write):
    for g, w in enumerate(POOL_WINDOWS):
        cols = slice(g * POOL_GROUP, (g + 1) * POOL_GROUP)
        tot = None
        for s in range(w):
            part = ext_ref[lead + (pl.ds(_HALO - s, t), cols)]
            tot = part if tot is None else tot + part
        cnt = jnp.minimum(pos + 1, w).astype(F32)
        pooled = tot / cnt - ext_ref[lead + (pl.ds(_HALO, t), cols)]
        write(g, cols, pooled)


def _pool_prompt_kernel(u_ref, w_ref, scale_ref, init_ref, o_ref, ext_s, wq_s):
    del init_ref
    i = pl.program_id(1)
    t = u_ref.shape[0]

    @pl.when((pl.program_id(0) == 0) & (i == 0))
    def _():
        wq_s[...] = w_ref[...].astype(BF16)

    @pl.when(i == 0)
    def _():
        ext_s[0:_HALO, :] = jnp.zeros((_HALO, POOL_WIDTH), F32)

    @pl.when(i != 0)
    def _():
        ext_s[0:_HALO, :] = ext_s[t:t + _HALO, :]

    ext_s[_HALO:_HALO + t, :] = u_ref[...]
    pos = i * t + lax.broadcasted_iota(jnp.int32, (t, 1), 0)

    def write(g, cols, pooled):
        y = jnp.dot(pooled.astype(BF16), wq_s[g], preferred_element_type=F32)
        o_ref[:, cols] = (y * scale_ref[:, cols]).astype(o_ref.dtype)

    _pool_groups(ext_s, (), t, pos, write)


def _pool_prompt(u, w_pool_mix, pool_scale):
    t = 256
    nt = SEQ // t
    return pl.pallas_call(
        _pool_prompt_kernel,
        grid=(BATCH, nt),
        in_specs=[pl.BlockSpec((t, POOL_WIDTH), lambda b, i: (b * nt + i, 0)),
                  pl.BlockSpec((len(POOL_WINDOWS), POOL_GROUP, POOL_GROUP), lambda b, i: (0, 0, 0)),
                  pl.BlockSpec((1, POOL_WIDTH), lambda b, i: (0, 0)), _HBM],
        out_specs=pl.BlockSpec((t, POOL_WIDTH), lambda b, i: (b * nt + i, 0)),
        out_shape=jax.ShapeDtypeStruct((M_ALL, POOL_WIDTH), BF16),
        input_output_aliases={3: 0},
        scratch_shapes=[pltpu.VMEM((_HALO + t, POOL_WIDTH), F32),
                        pltpu.VMEM((len(POOL_WINDOWS), POOL_GROUP, POOL_GROUP), BF16)],
        compiler_params=_params(("arbitrary", "arbitrary")),
    )(u, w_pool_mix, pool_scale.reshape(1, POOL_WIDTH), jnp.zeros((M_ALL, POOL_WIDTH), BF16))


def _pool_sample_kernel(state_ref, u_ref, w_ref, scale_ref, pooled_in, o_ref, ext_s):
    del pooled_in
    ext_s[:, 0:_HALO, :] = state_ref[...]
    ext_s[:, _HALO:_HALO + DEC_SEQ, :] = u_ref[...]
    pos = PAST_LEN + lax.broadcasted_iota(jnp.int32, (1, DEC_SEQ, 1), 1)

    def write(g, cols, pooled):
        x = pooled.reshape(M_SAMPLE, POOL_GROUP).astype(BF16)
        y = jnp.dot(x, w_ref[g].astype(BF16), preferred_element_type=F32)
        o_ref[:, cols] = (y * scale_ref[:, cols]).astype(o_ref.dtype)

    _pool_groups(ext_s, (slice(None),), DEC_SEQ, pos, write)


def _pool_sample(state16, u3, w_pool_mix, pool_scale, pooled_all):
    ng = len(POOL_WINDOWS)
    return pl.pallas_call(
        _pool_sample_kernel,
        grid=(1,),
        in_specs=[pl.BlockSpec((DEC_BATCH, _HALO, POOL_WIDTH), lambda i: (0, 0, 0)),
                  pl.BlockSpec((DEC_BATCH, DEC_SEQ, POOL_WIDTH), lambda i: (0, 0, 0)),
                  pl.BlockSpec((ng, POOL_GROUP, POOL_GROUP), lambda i: (0, 0, 0)),
                  pl.BlockSpec((1, POOL_WIDTH), lambda i: (0, 0)),
                  pl.BlockSpec(memory_space=pl.ANY)],
        out_specs=pl.BlockSpec((M_SAMPLE, POOL_WIDTH), lambda i: (M_PROMPT // M_SAMPLE, 0)),
        out_shape=jax.ShapeDtypeStruct((M_ALL, POOL_WIDTH), BF16),
        input_output_aliases={4: 0},
        scratch_shapes=[pltpu.VMEM((DEC_BATCH, _HALO + DEC_SEQ, POOL_WIDTH), F32)],
        compiler_params=_params(("arbitrary",)),
    )(state16, u3, w_pool_mix, pool_scale.reshape(1, POOL_WIDTH), pooled_all)


def _rope_tables():
    inv = 1.0 / (ROPE_THETA ** (jnp.arange(0, HALF_DIM, 2, dtype=F32) / HALF_DIM))
    pos = jnp.concatenate([jnp.tile(jnp.arange(SEQ), BATCH),
                           jnp.tile(PAST_LEN + jnp.arange(DEC_SEQ), DEC_BATCH)])
    ang = pos.astype(F32)[:, None] * inv[None, :]
    cos, sin = jnp.cos(ang), jnp.sin(ang)
    cos = jnp.tile(cos, (1, HEAD_DIM // (HALF_DIM // 2)))
    sin = jnp.tile(jnp.concatenate([-sin, sin], axis=1), (1, 2))
    return cos, sin


def _mod_rows(ada):
    ada = ada[:BATCH + DEC_BATCH].reshape(BATCH + DEC_BATCH, N_ADA, D_MODEL)
    prompt = jnp.broadcast_to(ada[:BATCH].transpose(1, 0, 2)[:, :, None, :],
                              (N_ADA, BATCH, ROW_TILE, D_MODEL))
    sample = jnp.repeat(ada[BATCH:], DEC_SEQ, axis=0)
    sample = sample.reshape(M_SAMPLE // ROW_TILE, ROW_TILE, N_ADA, D_MODEL).transpose(2, 0, 1, 3)
    return jnp.concatenate([prompt, sample], axis=1)


def kernel(x_prompt, x_sample, cache_k, cache_v, state_pool, page_table, c_prompt, c_sample,
           w_ada, b_ada, g_pre_ffn1, w1_ffn1, w3_ffn1, w2_ffn1, g_post_ffn1,
           g_pre_mix, w_in, lambda_q1, lambda_k1, lambda_q2, lambda_k2, subln_g,
           w_pool_mix, pool_scale, w_up_attn, w_up_pool, w_merge_gate, w_out, g_post_mix,
           g_pre_ffn2, w1_ffn2, w3_ffn2, w2_ffn2, g_post_ffn2):
    x = (x_prompt.reshape(M_PROMPT, D_MODEL), x_sample.reshape(M_SAMPLE, D_MODEL))
    c = jnp.concatenate([c_prompt, c_sample])
    c = jnp.pad(c, ((0, 48 - c.shape[0]), (0, 0)))
    mod = _mod_rows(_ada(c, w_ada, b_ada))
    lams = [v.reshape(1, HALF_DIM) for v in (lambda_q1, lambda_k1, lambda_q2, lambda_k2)]

    h = _norm_mod(x, g_pre_ffn1, mod, 0)
    f = _down(*_swiglu_up(h, w1_ffn1, w3_ffn1, w2_ffn1))
    x, h = _resid_norm_mod(x, f, g_post_ffn1, g_pre_mix, mod, 0, 1, 0.5)

    tabs = _rope_tables()
    q_scale = HALF_DIM ** -0.5 * LOG2_E
    q_p = _proj(h, w_in, 0, ATTN_WIDTH, BF16, _PROMPT_ROWS, tabs, q_scale)
    k_p = _proj(h, w_in, ATTN_WIDTH, ATTN_WIDTH, F32, _PROMPT_ROWS, tabs)
    v_p = _proj(h, w_in, 2 * ATTN_WIDTH, ATTN_WIDTH, F32, _PROMPT_ROWS)
    u_p = _proj(h, w_in, 3 * ATTN_WIDTH, POOL_WIDTH, F32, _PROMPT_ROWS)
    q_s = _proj(h, w_in, 0, ATTN_WIDTH, F32, _SAMPLE_ROWS, tabs, q_scale)
    k_s = _proj(h, w_in, ATTN_WIDTH, ATTN_WIDTH, F32, _SAMPLE_ROWS, tabs)
    v_s = _proj(h, w_in, 2 * ATTN_WIDTH, ATTN_WIDTH, F32, _SAMPLE_ROWS)
    u_s = _proj(h, w_in, 3 * ATTN_WIDTH, POOL_WIDTH, F32, _SAMPLE_ROWS)

    attn = _prompt_attn(q_p, k_p, v_p, lams, subln_g)
    attn = _sample_attn(q_s, k_s, v_s, cache_k, cache_v, page_table, lams, subln_g, attn)

    u_sample = u_s.reshape(DEC_BATCH, DEC_SEQ, POOL_WIDTH)
    state16 = jnp.pad(state_pool, ((0, 0), (_HALO - POOL_BUF, 0), (0, 0)))
    pooled = _pool_prompt(u_p, w_pool_mix, pool_scale)
    pooled = _pool_sample(state16, u_sample, w_pool_mix, pool_scale, pooled)

    merged = _merge(h, attn, pooled, w_merge_gate, w_up_attn, w_up_pool)
    m = _proj(merged, w_out, 0, D_MODEL, F32, _ALL_ROWS)
    x, h = _resid_norm_mod(x, m, g_post_mix, g_pre_ffn2, mod, 1, 2, 1.0)

    f = _down(*_swiglu_up(h, w1_ffn2, w3_ffn2, w2_ffn2))
    y_prompt, y_sample = _resid(x, f, g_post_ffn2, mod, 2, 0.5)
    y_prompt = y_prompt.reshape(BATCH, SEQ, D_MODEL)
    y_sample = y_sample.reshape(DEC_BATCH, DEC_SEQ, D_MODEL)
    k_prompt = k_p.reshape(BATCH, SEQ, N_HEADS, HEAD_DIM)
    v_prompt = v_p.reshape(BATCH, SEQ, N_HEADS, HEAD_DIM)
    k_sample = k_s.reshape(DEC_BATCH, DEC_SEQ, N_HEADS, HEAD_DIM)
    v_sample = v_s.reshape(DEC_BATCH, DEC_SEQ, N_HEADS, HEAD_DIM)
    pool_prompt = u_p.reshape(BATCH, SEQ, POOL_WIDTH)[:, SEQ - POOL_BUF:]
    pool_sample = jnp.concatenate([state_pool, u_sample], axis=1)[:, -POOL_BUF:]
    return (y_prompt, y_sample, k_prompt, v_prompt, pool_prompt, k_sample, v_sample, pool_sample)
```

```python
import functools
import math

import jax
import jax.numpy as jnp
from jax import lax
from jax.experimental import pallas as pl
from jax.experimental.pallas import tpu as pltpu

F32 = jnp.float32
BF16 = jnp.bfloat16

D_MODEL = 4096
BATCH = 2
SEQ = 4096
DEC_BATCH = 32
DEC_SEQ = 8
PAST_LEN = 8192
PAGE_SIZE = 128
N_PAGES = PAST_LEN // PAGE_SIZE
ATTN_WIDTH = D_MODEL // 2
HALF_DIM = 64
HEAD_DIM = 2 * HALF_DIM
N_HEADS = ATTN_WIDTH // HEAD_DIM
POOL_WIDTH = D_MODEL - ATTN_WIDTH
POOL_WINDOWS = (2, 4, 8, 16)
POOL_GROUP = POOL_WIDTH // len(POOL_WINDOWS)
POOL_BUF = max(POOL_WINDOWS) - 1
D_FF = ((8 * D_MODEL // 3 + 255) // 256) * 256
ROPE_THETA = 10000.0
N_ADA = 9
EPS = 1e-6
NEG = -1e30
LAM_INIT = 0.8 - 0.6 * math.exp(-0.3 * 0)

M_PROMPT = BATCH * SEQ
M_SAMPLE = DEC_BATCH * DEC_SEQ
M_ALL = M_PROMPT + M_SAMPLE

VMEM_LIMIT_BYTES = 58 * 1024 * 1024

ROW_TILE = 128
MM_ROWS = 768
STAGED_ROWS = 1056
MERGE_ROWS = 528
EPILOGUE_CHUNKS = 2
MERGE_CHUNKS = 3
LOG2_E = math.log2(math.e)
PAGES_PER_STEP = 8
HEADS_PER_GROUP = 8
Q_TILE = 1024
Q_ROW_BLOCKS = 4


def _params(sem):
    return pltpu.CompilerParams(dimension_semantics=sem, vmem_limit_bytes=VMEM_LIMIT_BYTES)


def _sigmoid(x):
    return 1.0 / (1.0 + jnp.exp(-x))


def _rms(x, g):
    return x * lax.rsqrt(jnp.mean(x * x, axis=-1, keepdims=True) + EPS) * g


def _ada_kernel(c_ref, w_ref, b_ref, o_ref):
    c = c_ref[...]
    a = (c * _sigmoid(c)).astype(BF16)
    o_ref[...] = jnp.dot(a, w_ref[...].astype(BF16), preferred_element_type=F32) + b_ref[...]


def _ada(c_pad, w_ada, b_ada):
    rows = c_pad.shape[0]
    n = w_ada.shape[1]
    tn = 1024
    return pl.pallas_call(
        _ada_kernel,
        grid=(n // tn,),
        in_specs=[pl.BlockSpec((rows, D_MODEL), lambda j: (0, 0)),
                  pl.BlockSpec((D_MODEL, tn), lambda j: (0, j)),
                  pl.BlockSpec((1, tn), lambda j: (0, j))],
        out_specs=pl.BlockSpec((rows, tn), lambda j: (0, j)),
        out_shape=jax.ShapeDtypeStruct((rows, n), F32),
        compiler_params=_params(("arbitrary",)),
    )(c_pad, w_ada, b_ada.reshape(1, n))


_PROMPT_TILES = M_PROMPT // ROW_TILE
_TILES_PER_BATCH = SEQ // ROW_TILE


def _mod_group(i):
    return jnp.where(i < _PROMPT_TILES, i // _TILES_PER_BATCH, i - _PROMPT_TILES + BATCH)


def _mod_spec(j):
    return pl.BlockSpec((1, 1, ROW_TILE, D_MODEL), lambda i: (j, _mod_group(i), 0, 0))


def _row_spec():
    return pl.BlockSpec((ROW_TILE, D_MODEL), lambda i: (i, 0))


def _pair_specs():
    return [pl.BlockSpec((ROW_TILE, D_MODEL), lambda i: (jnp.minimum(i, _PROMPT_TILES - 1), 0)),
            pl.BlockSpec((ROW_TILE, D_MODEL), lambda i: (jnp.maximum(i - _PROMPT_TILES, 0), 0))]


def _pair_shapes():
    return [jax.ShapeDtypeStruct((M_PROMPT, D_MODEL), F32),
            jax.ShapeDtypeStruct((M_SAMPLE, D_MODEL), F32)]


def _vec_spec():
    return pl.BlockSpec((1, D_MODEL), lambda i: (0, 0))


def _read_pair(xp_ref, xs_ref):
    return jnp.where(pl.program_id(0) < _PROMPT_TILES, xp_ref[...], xs_ref[...])


def _write_pair(xp_ref, xs_ref, x):
    @pl.when(pl.program_id(0) < _PROMPT_TILES)
    def _():
        xp_ref[...] = x

    @pl.when(pl.program_id(0) >= _PROMPT_TILES)
    def _():
        xs_ref[...] = x


def _norm_mod_kernel(xp_ref, xs_ref, g_ref, scale_ref, shift_ref, h_ref):
    h = _rms(_read_pair(xp_ref, xs_ref), g_ref[...])
    h_ref[...] = (h * (1.0 + scale_ref[0, 0]) + shift_ref[0, 0]).astype(h_ref.dtype)


def _norm_mod(x, g, mod, j):
    return pl.pallas_call(
        _norm_mod_kernel,
        grid=(M_ALL // ROW_TILE,),
        in_specs=_pair_specs() + [_vec_spec(), _mod_spec(3 * j + 1), _mod_spec(3 * j)],
        out_specs=_row_spec(),
        out_shape=jax.ShapeDtypeStruct((M_ALL, D_MODEL), BF16),
        compiler_params=_params(("arbitrary",)),
    )(*x, g.reshape(1, D_MODEL), mod, mod)


def _resid_norm_mod_kernel(xp_ref, xs_ref, f_ref, gpost_ref, gate_ref, gpre_ref, scale_ref,
                           shift_ref, xpo_ref, xso_ref, h_ref, *, coef):
    x = _read_pair(xp_ref, xs_ref) + coef * gate_ref[0, 0] * _rms(f_ref[...], gpost_ref[...])
    _write_pair(xpo_ref, xso_ref, x)
    h = _rms(x, gpre_ref[...])
    h_ref[...] = (h * (1.0 + scale_ref[0, 0]) + shift_ref[0, 0]).astype(h_ref.dtype)


def _resid_norm_mod(x, f, gpost, gpre, mod, j_prev, j_next, coef):
    xp, xs, h = pl.pallas_call(
        functools.partial(_resid_norm_mod_kernel, coef=coef),
        grid=(M_ALL // ROW_TILE,),
        in_specs=_pair_specs() + [_row_spec(), _vec_spec(), _mod_spec(3 * j_prev + 2), _vec_spec(),
                                  _mod_spec(3 * j_next + 1), _mod_spec(3 * j_next)],
        out_specs=_pair_specs() + [_row_spec()],
        out_shape=_pair_shapes() + [jax.ShapeDtypeStruct((M_ALL, D_MODEL), BF16)],
        compiler_params=_params(("arbitrary",)),
    )(*x, f, gpost.reshape(1, D_MODEL), mod, gpre.reshape(1, D_MODEL), mod, mod)
    return (xp, xs), h


def _resid_kernel(xp_ref, xs_ref, f_ref, gpost_ref, gate_ref, xpo_ref, xso_ref, *, coef):
    x = _read_pair(xp_ref, xs_ref) + coef * gate_ref[0, 0] * _rms(f_ref[...], gpost_ref[...])
    _write_pair(xpo_ref, xso_ref, x)


def _resid(x, f, gpost, mod, j_prev, coef):
    return pl.pallas_call(
        functools.partial(_resid_kernel, coef=coef),
        grid=(M_ALL // ROW_TILE,),
        in_specs=_pair_specs() + [_row_spec(), _vec_spec(), _mod_spec(3 * j_prev + 2)],
        out_specs=_pair_specs(),
        out_shape=_pair_shapes(),
        compiler_params=_params(("arbitrary",)),
    )(*x, f, gpost.reshape(1, D_MODEL), mod)


def _cast_kernel(w_ref, o_ref):
    o_ref[...] = w_ref[...].astype(o_ref.dtype)


def _first_tile(w, col0, tn):
    k = w.shape[0]
    joff = col0 // tn
    return pl.pallas_call(
        _cast_kernel,
        grid=(1,),
        in_specs=[pl.BlockSpec((k, tn), lambda i: (0, joff))],
        out_specs=pl.BlockSpec((k, tn), lambda i: (0, 0)),
        out_shape=jax.ShapeDtypeStruct((k, tn), BF16),
        compiler_params=_params(("arbitrary",)),
    )(w)


def _chunk_spec(w, col0, tn, nj, ni):
    k = w.shape[0]
    assert k % ni == 0 and (k // ni) % 8 == 0 and col0 % tn == 0
    joff = col0 // tn
    return pl.BlockSpec((k // ni, tn), lambda j, i: (i, jnp.minimum(j + 1, nj - 1) + joff))


def _stage_weights(chunk_refs, first_refs, slot_refs):
    j, i = pl.program_id(0), pl.program_id(1)

    @pl.when((j == 0) & (i == 0))
    def _():
        for first, slots in zip(first_refs, slot_refs):
            pltpu.sync_copy(first, slots.at[0])

    @pl.when(j + 1 < pl.num_programs(0))
    def _():
        nxt = (j + 1) % 2
        for chunk, slots in zip(chunk_refs, slot_refs):
            ck = chunk.shape[0]
            slots[nxt, pl.ds(pl.multiple_of(i * ck, ck), ck), :] = chunk[...].astype(BF16)

    return j % 2


def _slot_scratch(w, tn):
    return pltpu.VMEM((2, w.shape[0], tn), BF16)


_HBM = pl.BlockSpec(memory_space=pl.ANY)


def _swiglu_up_kernel(a_ref, w1c, w3c, w1f, w3f, w2_ref, o_ref, w2q_ref, w1q, w3q):
    slot = _stage_weights((w1c, w3c), (w1f, w3f), (w1q, w3q))
    w2q_ref[...] = w2_ref[...].astype(BF16)
    rows = a_ref.shape[0] // EPILOGUE_CHUNKS
    for c in range(EPILOGUE_CHUNKS):
        sl = pl.ds(c * rows, rows)
        a = a_ref[sl, :]
        u = jnp.dot(a, w1q[slot], preferred_element_type=F32)
        v = jnp.dot(a, w3q[slot], preferred_element_type=F32)
        o_ref[sl, :] = (u * _sigmoid(u) * v).astype(o_ref.dtype)


def _swiglu_cols(h, w1, w3, w2, col0, n, tn):
    tm = STAGED_ROWS
    ni, nj = M_ALL // tm, n // tn
    rb = n // (nj * ni)
    assert n % tn == 0 and M_ALL % tm == 0 and n % (nj * ni) == 0 and rb % 16 == 0
    roff = col0 // rb
    return pl.pallas_call(
        _swiglu_up_kernel,
        grid=(nj, ni),
        in_specs=[pl.BlockSpec((tm, D_MODEL), lambda j, i: (i, 0)),
                  _chunk_spec(w1, col0, tn, nj, ni), _chunk_spec(w3, col0, tn, nj, ni),
                  _HBM, _HBM,
                  pl.BlockSpec((rb, D_MODEL), lambda j, i: (roff + j * ni + i, 0))],
        out_specs=[pl.BlockSpec((tm, tn), lambda j, i: (i, j)),
                   pl.BlockSpec((rb, D_MODEL), lambda j, i: (j * ni + i, 0))],
        out_shape=[jax.ShapeDtypeStruct((M_ALL, n), BF16),
                   jax.ShapeDtypeStruct((n, D_MODEL), BF16)],
        scratch_shapes=[_slot_scratch(w1, tn), _slot_scratch(w3, tn)],
        compiler_params=_params(("arbitrary", "arbitrary")),
    )(h, w1, w3, _first_tile(w1, col0, tn), _first_tile(w3, col0, tn), w2)


def _swiglu_up(h, w1, w3, w2):
    wide = (D_FF // 512) * 512
    g_wide, w2_wide = _swiglu_cols(h, w1, w3, w2, 0, wide, 512)
    g_tail, w2_tail = _swiglu_cols(h, w1, w3, w2, wide, D_FF - wide, D_FF - wide)
    return (g_wide, g_tail), (w2_wide, w2_tail)


def _down_kernel(a0_ref, a1_ref, at_ref, w0_ref, w1_ref, wt_ref, o_ref):
    k = pl.program_id(1)
    last = pl.num_programs(1) - 1

    def pair():
        return (jnp.dot(a0_ref[...], w0_ref[...], preferred_element_type=F32)
                + jnp.dot(a1_ref[...], w1_ref[...], preferred_element_type=F32))

    @pl.when(k == 0)
    def _():
        o_ref[...] = pair()

    @pl.when((k != 0) & (k != last))
    def _():
        o_ref[...] += pair()

    @pl.when(k == last)
    def _():
        o_ref[...] += jnp.dot(at_ref[...], wt_ref[...], preferred_element_type=F32)


def _down(g, w2q):
    g_wide, g_tail = g
    w_wide, w_tail = w2q
    tk = g_tail.shape[1]
    tm = MM_ROWS
    nk = D_FF // tk
    nw = g_wide.shape[1] // tk
    assert nk % 2 == 1 and nw == nk - 1 and w_wide.shape[0] == nw * tk and w_tail.shape[0] == tk

    def even(k):
        return jnp.minimum(2 * k, nw - 2)

    def odd(k):
        return jnp.minimum(2 * k + 1, nw - 1)

    return pl.pallas_call(
        _down_kernel,
        grid=(M_ALL // tm, (nk + 1) // 2),
        in_specs=[pl.BlockSpec((tm, tk), lambda i, k: (i, even(k))),
                  pl.BlockSpec((tm, tk), lambda i, k: (i, odd(k))),
                  pl.BlockSpec((tm, tk), lambda i, k: (i, 0)),
                  pl.BlockSpec((tk, D_MODEL), lambda i, k: (even(k), 0)),
                  pl.BlockSpec((tk, D_MODEL), lambda i, k: (odd(k), 0)),
                  pl.BlockSpec((tk, D_MODEL), lambda i, k: (0, 0))],
        out_specs=pl.BlockSpec((tm, D_MODEL), lambda i, k: (i, 0)),
        out_shape=jax.ShapeDtypeStruct((M_ALL, D_MODEL), F32),
        compiler_params=_params(("arbitrary", "arbitrary")),
    )(g_wide, g_wide, g_tail, w_wide, w_wide, w_tail)


def _proj_kernel(a_ref, wc, wf, *rest, rope, out_scale):
    if rope:
        cos_ref, sin_ref, o_ref, wq = rest
    else:
        o_ref, wq = rest
    slot = _stage_weights((wc,), (wf,), (wq,))
    rows = a_ref.shape[0] // EPILOGUE_CHUNKS
    tn = o_ref.shape[1]
    for c in range(EPILOGUE_CHUNKS):
        sl = pl.ds(c * rows, rows)
        y = jnp.dot(a_ref[sl, :], wq[slot], preferred_element_type=F32)
        if rope:
            reps = tn // HEAD_DIM
            cos = jnp.tile(cos_ref[sl, :], (1, reps))
            sin = jnp.tile(sin_ref[sl, :], (1, reps))
            lane = lax.broadcasted_iota(jnp.int32, y.shape, 1)
            first = (lane % HALF_DIM) < (HALF_DIM // 2)
            partner = jnp.where(first, pltpu.roll(y, tn - HALF_DIM // 2, 1),
                                pltpu.roll(y, HALF_DIM // 2, 1))
            y = y * cos + partner * sin
        if out_scale != 1.0:
            y = y * out_scale
        o_ref[sl, :] = y.astype(o_ref.dtype)


def _proj(a, w, col0, n, out_dtype, rows, rope_tabs=None, out_scale=1.0):
    row0, m, tm, tn = rows
    k = a.shape[1]
    ni, nj = m // tm, n // tn
    ioff = row0 // tm
    assert row0 % tm == 0 and m % tm == 0 and n % tn == 0
    in_specs = [pl.BlockSpec((tm, k), lambda j, i: (i + ioff, 0)),
                _chunk_spec(w, col0, tn, nj, ni), _HBM]
    args = [a, w, _first_tile(w, col0, tn)]
    if rope_tabs is not None:
        in_specs += [pl.BlockSpec((tm, HEAD_DIM), lambda j, i: (i + ioff, 0))] * 2
        args += list(rope_tabs)
    return pl.pallas_call(
        functools.partial(_proj_kernel, rope=rope_tabs is not None, out_scale=out_scale),
        grid=(nj, ni),
        in_specs=in_specs,
        out_specs=pl.BlockSpec((tm, tn), lambda j, i: (i, j)),
        out_shape=jax.ShapeDtypeStruct((m, n), out_dtype),
        scratch_shapes=[_slot_scratch(w, tn)],
        compiler_params=_params(("arbitrary", "arbitrary")),
    )(*args)


_ALL_ROWS = (0, M_ALL, STAGED_ROWS, 1024)
_PROMPT_ROWS = (0, M_PROMPT, 1024, 1024)
_SAMPLE_ROWS = (M_PROMPT, M_SAMPLE, M_SAMPLE, 512)


def _merge_kernel(h_ref, at_ref, po_ref, wg0c, wg1c, wac, wpc, wg0f, wg1f, waf, wpf, o_ref,
                  wg0q, wg1q, waq, wpq):
    slot = _stage_weights((wg0c, wg1c, wac, wpc), (wg0f, wg1f, waf, wpf), (wg0q, wg1q, waq, wpq))
    rows = h_ref.shape[0] // MERGE_CHUNKS
    for c in range(MERGE_CHUNKS):
        sl = pl.ds(c * rows, rows)
        h = h_ref[sl, :]
        g0 = _sigmoid(jnp.dot(h, wg0q[slot], preferred_element_type=F32))
        g1 = _sigmoid(jnp.dot(h, wg1q[slot], preferred_element_type=F32))
        a_up = jnp.dot(at_ref[sl, :], waq[slot], preferred_element_type=F32)
        p_up = jnp.dot(po_ref[sl, :], wpq[slot], preferred_element_type=F32)
        o_ref[sl, :] = (g0 * a_up + g1 * p_up).astype(o_ref.dtype)


def _merge(h, attn, pooled, w_gate, w_up_attn, w_up_pool):
    tn = 512
    tm = MERGE_ROWS
    ni, nj = M_ALL // tm, D_MODEL // tn
    cols = [(w_gate, 0), (w_gate, D_MODEL), (w_up_attn, 0), (w_up_pool, 0)]
    return pl.pallas_call(
        _merge_kernel,
        grid=(nj, ni),
        in_specs=[pl.BlockSpec((tm, D_MODEL), lambda j, i: (i, 0)),
                  pl.BlockSpec((tm, ATTN_WIDTH), lambda j, i: (i, 0)),
                  pl.BlockSpec((tm, POOL_WIDTH), lambda j, i: (i, 0))]
                 + [_chunk_spec(w, c0, tn, nj, ni) for w, c0 in cols] + [_HBM] * len(cols),
        out_specs=pl.BlockSpec((tm, tn), lambda j, i: (i, j)),
        out_shape=jax.ShapeDtypeStruct((M_ALL, D_MODEL), BF16),
        scratch_shapes=[_slot_scratch(w, tn) for w, _ in cols],
        compiler_params=_params(("arbitrary", "arbitrary")),
    )(h, attn, pooled, *[w for w, _ in cols], *[_first_tile(w, c0, tn) for w, c0 in cols])


def _lambda(lq1, lk1, lq2, lk2):
    return (jnp.exp(jnp.sum(lq1[...] * lk1[...], axis=-1, keepdims=True))
            - jnp.exp(jnp.sum(lq2[...] * lk2[...], axis=-1, keepdims=True)) + LAM_INIT)


def _split_halves(q):
    lane = lax.broadcasted_iota(jnp.int32, q.shape, 1)
    zero = jnp.zeros_like(q)
    return jnp.concatenate([jnp.where(lane < HALF_DIM, q, zero),
                            jnp.where(lane >= HALF_DIM, q, zero)], axis=0)


def _head_out(o, lam, g):
    t = o.shape[0] // 2
    a = o[:t] - lam * o[t:]
    return _rms(a, g) * (1.0 - LAM_INIT)


def _online_update(m_ref, acc_ref, idx, s, v_ones):
    m_prev = m_ref[idx]
    m_new = jnp.maximum(m_prev, jnp.max(s, axis=-1, keepdims=True))
    p = jnp.exp2(s - jnp.tile(m_new, (1, s.shape[1] // HEAD_DIM)))
    alpha = jnp.exp2(m_prev - m_new)
    acc_ref[idx] = (acc_ref[idx] * jnp.tile(alpha, (1, 2))
                    + jnp.dot(p.astype(BF16), v_ones, preferred_element_type=F32))
    m_ref[idx] = m_new


def _prompt_attn_kernel(q_ref, k_ref, v_ref, lq1, lk1, lq2, lk2, g_ref, init_ref, o_ref,
                        kb, vb, m_s, acc_s, s_s):
    del init_ref
    tq = Q_TILE
    kb[...] = k_ref[...].astype(BF16)
    vb[:, :HEAD_DIM] = v_ref[...].astype(BF16)
    vb[:, HEAD_DIM:] = jnp.ones((SEQ, HEAD_DIM), BF16)
    lam = _lambda(lq1, lk1, lq2, lk2)
    g = g_ref[...]

    n_chains, rows = m_s.shape[0], m_s.shape[1]
    row_blocks = n_chains // 2

    def q_tile(qi, carry):
        q0 = pl.multiple_of(qi * tq, tq)
        chains = []
        for c in range(n_chains):
            r0 = (c % row_blocks) * rows
            q = q_ref[pl.ds(q0 + r0, rows), :]
            lane = lax.broadcasted_iota(jnp.int32, q.shape, 1)
            keep = (lane < HALF_DIM) if c < row_blocks else (lane >= HALF_DIM)
            chains.append((r0, jnp.where(keep, q, jnp.zeros_like(q))))
        m_s[...] = jnp.full(m_s.shape, NEG, F32)
        acc_s[...] = jnp.zeros(acc_s.shape, F32)

        def scores(j, slot):
            k = kb[pl.ds(pl.multiple_of(j * tq, tq), tq), :]
            for c, (_, qc) in enumerate(chains):
                s_s[slot, c] = lax.dot_general(qc, k, (((1,), (1,)), ((), ())),
                                               preferred_element_type=F32)

        scores(0, 0)

        def body(j, inner):
            for slot in range(2):
                @pl.when(j % 2 == slot)
                def _():
                    scores(j + 1, 1 - slot)
                    v = vb[pl.ds(pl.multiple_of(j * tq, tq), tq), :]
                    for c in range(n_chains):
                        _online_update(m_s, acc_s, (c,), s_s[slot, c], v)
            return inner

        lax.fori_loop(0, qi, body, 0)

        v = vb[pl.ds(q0, tq), :]
        for c, (r0, _) in enumerate(chains):
            nk = r0 + rows
            s = s_s[qi % 2, c][:, :nk]
            row = r0 + lax.broadcasted_iota(jnp.int32, s.shape, 0)
            col = lax.broadcasted_iota(jnp.int32, s.shape, 1)
            _online_update(m_s, acc_s, (c,), jnp.where(col <= row, s, NEG), v[:nk])

        for rb in range(row_blocks):
            acc = jnp.concatenate([acc_s[rb], acc_s[row_blocks + rb]], axis=0)
            o = acc[:, :HEAD_DIM] / acc[:, HEAD_DIM:]
            o_ref[pl.ds(q0 + rb * rows, rows), :] = _head_out(o, lam, g).astype(o_ref.dtype)
        return carry

    lax.fori_loop(0, SEQ // tq, q_tile, 0)


def _prompt_attn(q, k, v, lams, subln_g):
    tq = Q_TILE
    small = pl.BlockSpec((1, HALF_DIM), lambda b, h: (0, 0))
    head = pl.BlockSpec((SEQ, HEAD_DIM), lambda b, h: (b, h))
    return pl.pallas_call(
        _prompt_attn_kernel,
        grid=(BATCH, N_HEADS),
        in_specs=[head, head, head, small, small, small, small,
                  pl.BlockSpec((1, HEAD_DIM), lambda b, h: (0, 0)), _HBM],
        out_specs=head,
        out_shape=jax.ShapeDtypeStruct((M_ALL, ATTN_WIDTH), BF16),
        input_output_aliases={8: 0},
        scratch_shapes=[pltpu.VMEM((SEQ, HEAD_DIM), BF16), pltpu.VMEM((SEQ, 2 * HEAD_DIM), BF16),
                        pltpu.VMEM((2 * Q_ROW_BLOCKS, tq // Q_ROW_BLOCKS, HEAD_DIM), F32),
                        pltpu.VMEM((2 * Q_ROW_BLOCKS, tq // Q_ROW_BLOCKS, 2 * HEAD_DIM), F32),
                        pltpu.VMEM((2, 2 * Q_ROW_BLOCKS, tq // Q_ROW_BLOCKS, tq), F32)],
        compiler_params=_params(("arbitrary", "arbitrary")),
    )(q, k, v, *lams, subln_g.reshape(1, HEAD_DIM), jnp.zeros((M_ALL, ATTN_WIDTH), BF16))


_GROUP_ROWS = HEADS_PER_GROUP * 2 * DEC_SEQ
_SLAB_ROWS = PAGE_SIZE * HEADS_PER_GROUP
_N_GROUPS = N_HEADS // HEADS_PER_GROUP


def _sample_attn_kernel(pt_ref, q_ref, kn_ref, vn_ref, *rest):
    del pt_ref
    kpages = rest[:PAGES_PER_STEP]
    vpages = rest[PAGES_PER_STEP:2 * PAGES_PER_STEP]
    (lq1, lk1, lq2, lk2, g_ref, attn_in, o_ref,
     qz_s, bias_s, m_s, acc_s, out_s) = rest[2 * PAGES_PER_STEP:]
    del attn_in
    b = pl.program_id(0)
    step = pl.program_id(1)
    last_step = pl.num_programs(1) - 1

    @pl.when(step == 0)
    def _():
        for h in range(N_HEADS):
            grp, hl = divmod(h, HEADS_PER_GROUP)
            qh = q_ref[:, h * HEAD_DIM:(h + 1) * HEAD_DIM]
            qz_s[grp, hl * 2 * DEC_SEQ:(hl + 1) * 2 * DEC_SEQ, :] = _split_halves(qh)
        m_s[...] = jnp.full(m_s.shape, NEG, F32)
        acc_s[...] = jnp.zeros(acc_s.shape, F32)

    @pl.when((b == 0) & (step == 0))
    def _():
        row = lax.broadcasted_iota(jnp.int32, bias_s.shape, 0)
        col = lax.broadcasted_iota(jnp.int32, bias_s.shape, 1)
        bias_s[...] = jnp.where(col % HEADS_PER_GROUP == row // (2 * DEC_SEQ), 0.0, NEG)

    ones = jnp.ones((PAGES_PER_STEP * _SLAB_ROWS, HEAD_DIM), BF16)
    for grp in range(_N_GROUPS):
        lo = grp * HEADS_PER_GROUP
        qz = qz_s[grp].astype(BF16)
        k2 = jnp.concatenate(
            [kp[0, :, lo:lo + HEADS_PER_GROUP, :].reshape(_SLAB_ROWS, HEAD_DIM).astype(BF16)
             for kp in kpages], axis=0)
        v2 = jnp.concatenate(
            [vp[0, :, lo:lo + HEADS_PER_GROUP, :].reshape(_SLAB_ROWS, HEAD_DIM).astype(BF16)
             for vp in vpages], axis=0)
        s = lax.dot_general(qz, k2, (((1,), (1,)), ((), ())), preferred_element_type=F32)
        s = s + bias_s[...]
        _online_update(m_s, acc_s, (grp,), s, jnp.concatenate([v2, ones], axis=1))

    @pl.when(step == last_step)
    def _():
        lam = _lambda(lq1, lk1, lq2, lk2)
        g = g_ref[...]
        out_rows = pl.ds(pl.multiple_of(b * DEC_SEQ, DEC_SEQ), DEC_SEQ)
        for h in range(N_HEADS):
            grp, hl = divmod(h, HEADS_PER_GROUP)
            rows = pl.ds(hl * 2 * DEC_SEQ, 2 * DEC_SEQ)
            cols = slice(h * HEAD_DIM, (h + 1) * HEAD_DIM)
            qh = qz_s[grp, rows, :]
            s = lax.dot_general(qh, kn_ref[:, cols], (((1,), (1,)), ((), ())),
                                preferred_element_type=F32)
            row = lax.broadcasted_iota(jnp.int32, s.shape, 0)
            col = lax.broadcasted_iota(jnp.int32, s.shape, 1)
            causal = col <= row % DEC_SEQ
            s = jnp.where(causal, s, NEG)
            m_prev = m_s[grp, rows, :]
            m_new = jnp.maximum(m_prev, jnp.max(s, axis=-1, keepdims=True))
            p = jnp.where(causal, jnp.exp2(s - m_new[:, :DEC_SEQ]), 0.0)
            alpha = jnp.exp2(m_prev - m_new)
            acc = acc_s[grp, rows, :]
            num = acc[:, :HEAD_DIM] * alpha + jnp.dot(p, vn_ref[:, cols],
                                                      preferred_element_type=F32)
            den = acc[:, HEAD_DIM:] * alpha + jnp.sum(p, axis=-1, keepdims=True)
            out_s[out_rows, cols] = _head_out(num / den, lam, g)

    @pl.when((b == pl.num_programs(0) - 1) & (step == last_step))
    def _():
        o_ref[...] = out_s[...].astype(o_ref.dtype)


def _sample_attn(q, k, v, cache_k, cache_v, page_table, lams, subln_g, attn_all):
    pps = PAGES_PER_STEP

    def new_spec():
        return pl.BlockSpec((DEC_SEQ, ATTN_WIDTH), lambda b, s, pt: (b, 0))

    def page_spec(p):
        return pl.BlockSpec((1, PAGE_SIZE, N_HEADS, HEAD_DIM),
                            lambda b, s, pt: (pt[b, s * pps + p], 0, 0, 0))

    small = pl.BlockSpec((1, HALF_DIM), lambda b, s, pt: (0, 0))
    in_specs = ([new_spec(), new_spec(), new_spec()]
                + [page_spec(p) for p in range(pps)] + [page_spec(p) for p in range(pps)]
                + [small, small, small, small,
                   pl.BlockSpec((1, HEAD_DIM), lambda b, s, pt: (0, 0)),
                   pl.BlockSpec(memory_space=pl.ANY)])
    grid_spec = pltpu.PrefetchScalarGridSpec(
        num_scalar_prefetch=1,
        grid=(DEC_BATCH, N_PAGES // pps),
        in_specs=in_specs,
        out_specs=pl.BlockSpec((M_SAMPLE, ATTN_WIDTH),
                               lambda b, s, pt: (M_PROMPT // M_SAMPLE, 0)),
        scratch_shapes=[pltpu.VMEM((_N_GROUPS, _GROUP_ROWS, HEAD_DIM), F32),
                        pltpu.VMEM((_GROUP_ROWS, pps * _SLAB_ROWS), F32),
                        pltpu.VMEM((_N_GROUPS, _GROUP_ROWS, HEAD_DIM), F32),
                        pltpu.VMEM((_N_GROUPS, _GROUP_ROWS, 2 * HEAD_DIM), F32),
                        pltpu.VMEM((M_SAMPLE, ATTN_WIDTH), F32)],
    )
    return pl.pallas_call(
        _sample_attn_kernel,
        grid_spec=grid_spec,
        out_shape=jax.ShapeDtypeStruct((M_ALL, ATTN_WIDTH), BF16),
        input_output_aliases={len(in_specs): 0},
        compiler_params=_params(("arbitrary", "arbitrary")),
    )(page_table, q, k, v, *([cache_k] * pps), *([cache_v] * pps), *lams,
      subln_g.reshape(1, HEAD_DIM), attn_all)


_HALO = 16


def _pool_groups(ext_ref, lead, t, pos, write):
    for g, w in enumerate(POOL_WINDOWS):
        cols = slice(g * POOL_GROUP, (g + 1) * POOL_GROUP)
        tot = None
        for s in range(w):
            part = ext_ref[lead + (pl.ds(_HALO - s, t), cols)]
            tot = part if tot is None else tot + part
        cnt = jnp.minimum(pos + 1, w).astype(F32)
        pooled = tot / cnt - ext_ref[lead + (pl.ds(_HALO, t), cols)]
        write(g, cols, pooled)


def _pool_prompt_kernel(u_ref, w_ref, scale_ref, init_ref, o_ref, ext_s, wq_s):
    del init_ref
    i = pl.program_id(1)
    t = u_ref.shape[0]

    @pl.when((pl.program_id(0) == 0) & (i == 0))
    def _():
        wq_s[...] = w_ref[...].astype(BF16)

    @pl.when(i == 0)
    def _():
        ext_s[0:_HALO, :] = jnp.zeros((_HALO, POOL_WIDTH), F32)

    @pl.when(i != 0)
    def _():
        ext_s[0:_HALO, :] = ext_s[t:t + _HALO, :]

    ext_s[_HALO:_HALO + t, :] = u_ref[...]
    pos = i * t + lax.broadcasted_iota(jnp.int32, (t, 1), 0)

    def write(g, cols, pooled):
        y = jnp.dot(pooled.astype(BF16), wq_s[g], preferred_element_type=F32)
        o_ref[:, cols] = (y * scale_ref[:, cols]).astype(o_ref.dtype)

    _pool_groups(ext_s, (), t, pos, write)


def _pool_prompt(u, w_pool_mix, pool_scale):
    t = 256
    nt = SEQ // t
    return pl.pallas_call(
        _pool_prompt_kernel,
        grid=(BATCH, nt),
        in_specs=[pl.BlockSpec((t, POOL_WIDTH), lambda b, i: (b * nt + i, 0)),
                  pl.BlockSpec((len(POOL_WINDOWS), POOL_GROUP, POOL_GROUP), lambda b, i: (0, 0, 0)),
                  pl.BlockSpec((1, POOL_WIDTH), lambda b, i: (0, 0)), _HBM],
        out_specs=pl.BlockSpec((t, POOL_WIDTH), lambda b, i: (b * nt + i, 0)),
        out_shape=jax.ShapeDtypeStruct((M_ALL, POOL_WIDTH), BF16),
        input_output_aliases={3: 0},
        scratch_shapes=[pltpu.VMEM((_HALO + t, POOL_WIDTH), F32),
                        pltpu.VMEM((len(POOL_WINDOWS), POOL_GROUP, POOL_GROUP), BF16)],
        compiler_params=_params(("arbitrary", "arbitrary")),
    )(u, w_pool_mix, pool_scale.reshape(1, POOL_WIDTH), jnp.zeros((M_ALL, POOL_WIDTH), BF16))


def _pool_sample_kernel(state_ref, u_ref, w_ref, scale_ref, pooled_in, o_ref, ext_s):
    del pooled_in
    ext_s[:, 0:_HALO, :] = state_ref[...]
    ext_s[:, _HALO:_HALO + DEC_SEQ, :] = u_ref[...]
    pos = PAST_LEN + lax.broadcasted_iota(jnp.int32, (1, DEC_SEQ, 1), 1)

    def write(g, cols, pooled):
        x = pooled.reshape(M_SAMPLE, POOL_GROUP).astype(BF16)
        y = jnp.dot(x, w_ref[g].astype(BF16), preferred_element_type=F32)
        o_ref[:, cols] = (y * scale_ref[:, cols]).astype(o_ref.dtype)

    _pool_groups(ext_s, (slice(None),), DEC_SEQ, pos, write)


def _pool_sample(state16, u3, w_pool_mix, pool_scale, pooled_all):
    ng = len(POOL_WINDOWS)
    return pl.pallas_call(
        _pool_sample_kernel,
        grid=(1,),
        in_specs=[pl.BlockSpec((DEC_BATCH, _HALO, POOL_WIDTH), lambda i: (0, 0, 0)),
                  pl.BlockSpec((DEC_BATCH, DEC_SEQ, POOL_WIDTH), lambda i: (0, 0, 0)),
                  pl.BlockSpec((ng, POOL_GROUP, POOL_GROUP), lambda i: (0, 0, 0)),
                  pl.BlockSpec((1, POOL_WIDTH), lambda i: (0, 0)),
                  pl.BlockSpec(memory_space=pl.ANY)],
        out_specs=pl.BlockSpec((M_SAMPLE, POOL_WIDTH), lambda i: (M_PROMPT // M_SAMPLE, 0)),
        out_shape=jax.ShapeDtypeStruct((M_ALL, POOL_WIDTH), BF16),
        input_output_aliases={4: 0},
        scratch_shapes=[pltpu.VMEM((DEC_BATCH, _HALO + DEC_SEQ, POOL_WIDTH), F32)],
        compiler_params=_params(("arbitrary",)),
    )(state16, u3, w_pool_mix, pool_scale.reshape(1, POOL_WIDTH), pooled_all)


def _rope_tables():
    inv = 1.0 / (ROPE_THETA ** (jnp.arange(0, HALF_DIM, 2, dtype=F32) / HALF_DIM))
    pos = jnp.concatenate([jnp.tile(jnp.arange(SEQ), BATCH),
                           jnp.tile(PAST_LEN + jnp.arange(DEC_SEQ), DEC_BATCH)])
    ang = pos.astype(F32)[:, None] * inv[None, :]
    cos, sin = jnp.cos(ang), jnp.sin(ang)
    cos = jnp.tile(cos, (1, HEAD_DIM // (HALF_DIM // 2)))
    sin = jnp.tile(jnp.concatenate([-sin, sin], axis=1), (1, 2))
    return cos, sin


def _mod_rows(ada):
    ada = ada[:BATCH + DEC_BATCH].reshape(BATCH + DEC_BATCH, N_ADA, D_MODEL)
    prompt = jnp.broadcast_to(ada[:BATCH].transpose(1, 0, 2)[:, :, None, :],
                              (N_ADA, BATCH, ROW_TILE, D_MODEL))
    sample = jnp.repeat(ada[BATCH:], DEC_SEQ, axis=0)
    sample = sample.reshape(M_SAMPLE // ROW_TILE, ROW_TILE, N_ADA, D_MODEL).transpose(2, 0, 1, 3)
    return jnp.concatenate([prompt, sample], axis=1)


def kernel(x_prompt, x_sample, cache_k, cache_v, state_pool, page_table, c_prompt, c_sample,
           w_ada, b_ada, g_pre_ffn1, w1_ffn1, w3_ffn1, w2_ffn1, g_post_ffn1,
           g_pre_mix, w_in, lambda_q1, lambda_k1, lambda_q2, lambda_k2, subln_g,
           w_pool_mix, pool_scale, w_up_attn, w_up_pool, w_merge_gate, w_out, g_post_mix,
           g_pre_ffn2, w1_ffn2, w3_ffn2, w2_ffn2, g_post_ffn2):
    x = (x_prompt.reshape(M_PROMPT, D_MODEL), x_sample.reshape(M_SAMPLE, D_MODEL))
    c = jnp.concatenate([c_prompt, c_sample])
    c = jnp.pad(c, ((0, 48 - c.shape[0]), (0, 0)))
    mod = _mod_rows(_ada(c, w_ada, b_ada))
    lams = [v.reshape(1, HALF_DIM) for v in (lambda_q1, lambda_k1, lambda_q2, lambda_k2)]

    h = _norm_mod(x, g_pre_ffn1, mod, 0)
    f = _down(*_swiglu_up(h, w1_ffn1, w3_ffn1, w2_ffn1))
    x, h = _resid_norm_mod(x, f, g_post_ffn1, g_pre_mix, mod, 0, 1, 0.5)

    tabs = _rope_tables()
    q_scale = HALF_DIM ** -0.5 * LOG2_E
    q_p = _proj(h, w_in, 0, ATTN_WIDTH, BF16, _PROMPT_ROWS, tabs, q_scale)
    k_p = _proj(h, w_in, ATTN_WIDTH, ATTN_WIDTH, F32, _PROMPT_ROWS, tabs)
    v_p = _proj(h, w_in, 2 * ATTN_WIDTH, ATTN_WIDTH, F32, _PROMPT_ROWS)
    u_p = _proj(h, w_in, 3 * ATTN_WIDTH, POOL_WIDTH, F32, _PROMPT_ROWS)
    q_s = _proj(h, w_in, 0, ATTN_WIDTH, F32, _SAMPLE_ROWS, tabs, q_scale)
    k_s = _proj(h, w_in, ATTN_WIDTH, ATTN_WIDTH, F32, _SAMPLE_ROWS, tabs)
    v_s = _proj(h, w_in, 2 * ATTN_WIDTH, ATTN_WIDTH, F32, _SAMPLE_ROWS)
    u_s = _proj(h, w_in, 3 * ATTN_WIDTH, POOL_WIDTH, F32, _SAMPLE_ROWS)

    attn = _prompt_attn(q_p, k_p, v_p, lams, subln_g)
    attn = _sample_attn(q_s, k_s, v_s, cache_k, cache_v, page_table, lams, subln_g, attn)

    u_sample = u_s.reshape(DEC_BATCH, DEC_SEQ, POOL_WIDTH)
    state16 = jnp.pad(state_pool, ((0, 0), (_HALO - POOL_BUF, 0), (0, 0)))
    pooled = _pool_prompt(u_p, w_pool_mix, pool_scale)
    pooled = _pool_sample(state16, u_sample, w_pool_mix, pool_scale, pooled)

    merged = _merge(h, attn, pooled, w_merge_gate, w_up_attn, w_up_pool)
    m = _proj(merged, w_out, 0, D_MODEL, F32, _ALL_ROWS)
    x, h = _resid_norm_mod(x, m, g_post_mix, g_pre_ffn2, mod, 1, 2, 1.0)

    f = _down(*_swiglu_up(h, w1_ffn2, w3_ffn2, w2_ffn2))
    y_prompt, y_sample = _resid(x, f, g_post_ffn2, mod, 2, 0.5)
    y_prompt = y_prompt.reshape(BATCH, SEQ, D_MODEL)
    y_sample = y_sample.reshape(DEC_BATCH, DEC_SEQ, D_MODEL)
    k_prompt = k_p.reshape(BATCH, SEQ, N_HEADS, HEAD_DIM)
    v_prompt = v_p.reshape(BATCH, SEQ, N_HEADS, HEAD_DIM)
    k_sample = k_s.reshape(DEC_BATCH, DEC_SEQ, N_HEADS, HEAD_DIM)
    v_sample = v_s.reshape(DEC_BATCH, DEC_SEQ, N_HEADS, HEAD_DIM)
    pool_prompt = u_p.reshape(BATCH, SEQ, POOL_WIDTH)[:, SEQ - POOL_BUF:]
    pool_sample = jnp.concatenate([state_pool, u_sample], axis=1)[:, -POOL_BUF:]
    return (y_prompt, y_sample, k_prompt, v_prompt, pool_prompt, k_sample, v_sample, pool_sample)
```
